```python
import jax, jax.numpy as jnp
from jax import lax
import numpy as np

D_MODEL = 1024
BATCH = 8
SEQ = 4096
DEPTH = 1

CHUNK = 64

CONV_GROUPS = 8
CONV_GROUP_DIM = 64
CONV_WIDTH = CONV_GROUPS * CONV_GROUP_DIM
CONV_K = 3
ATTN_HEADS = 8
HEAD_DIM = 64
ATTN_WIDTH = ATTN_HEADS * HEAD_DIM
MIX_WIDTH = CONV_WIDTH + ATTN_WIDTH
IN_COLS = 3 * CONV_WIDTH + 3 * ATTN_WIDTH
D_FF = 4 * D_MODEL
Q_BLOCK = 128
ALPHA = float((2 * DEPTH) ** 0.25)
BETA = float((8 * DEPTH) ** -0.25)
LN_EPS = 1e-5
RMS_EPS = 1e-6

kernel_name = "hybrid_shortconv_stickbreaking_deepnorm_block"


def layer_norm(x, g, b):
    xf = x.astype(jnp.float32)
    mu = jnp.mean(xf, axis=-1, keepdims=True)
    var = jnp.mean(jnp.square(xf - mu), axis=-1, keepdims=True)
    y = (xf - mu) * lax.rsqrt(var + LN_EPS)
    return (y * g.astype(jnp.float32) + b.astype(jnp.float32)).astype(x.dtype)


def group_rmsnorm(y, g, n_groups):
    bsz, seq, w = y.shape
    yf = y.astype(jnp.float32).reshape(bsz, seq, n_groups, w // n_groups)
    yf = yf * lax.rsqrt(jnp.mean(jnp.square(yf), axis=-1, keepdims=True) + RMS_EPS)
    return (yf.reshape(bsz, seq, w) * g.astype(jnp.float32)).astype(y.dtype)


def short_gated_conv(b_gate, c_gate, h, w_conv):
    seq = h.shape[1]
    u = c_gate * h
    u_pad = jnp.pad(u, ((0, 0), (CONV_K - 1, 0), (0, 0)))
    y = sum(w_conv[j] * u_pad[:, j:j + seq, :] for j in range(CONV_K))
    return b_gate * y


def stick_breaking_attention(q, k, v):
    bsz, seq, nh, dh = q.shape
    scale = dh ** -0.5
    qh = jnp.transpose(q, (0, 2, 1, 3))
    kh = jnp.transpose(k, (0, 2, 1, 3))
    vh = jnp.transpose(v, (0, 2, 1, 3))
    outs = []
    for i in range(seq // Q_BLOCK):
        q0 = i * Q_BLOCK
        kend = q0 + Q_BLOCK
        qb = qh[:, :, q0:kend]
        kb = kh[:, :, :kend]
        vb = vh[:, :, :kend]
        z = jnp.einsum('bhqd,bhkd->bhqk', qb, kb).astype(jnp.float32) * scale
        t_idx = q0 + jnp.arange(Q_BLOCK)[:, None]
        s_idx = jnp.arange(kend)[None, :]
        mask = s_idx < t_idx
        log_keep = jnp.where(mask, jax.nn.log_sigmoid(-z), 0.0)
        suffix = lax.cumsum(log_keep, axis=3, reverse=True) - log_keep
        a = jnp.where(mask, jnp.exp(jax.nn.log_sigmoid(z) + suffix), 0.0)
        outs.append(jnp.einsum('bhqk,bhkd->bhqd', a.astype(vb.dtype), vb))
    o = jnp.concatenate(outs, axis=2)
    return jnp.transpose(o, (0, 2, 1, 3)).reshape(bsz, seq, nh * dh)


def setup_inputs(seed: int = 0) -> dict:
    key = jax.random.key(seed)
    ks = jax.random.split(key, 13)
    f32 = jnp.float32
    x = jax.random.normal(ks[0], (BATCH, SEQ, D_MODEL), f32)
    w_in = jax.random.normal(ks[1], (DEPTH, D_MODEL, IN_COLS), f32) * D_MODEL ** -0.5
    conv_w = jax.random.normal(ks[2], (DEPTH, CONV_K, CONV_WIDTH), f32) * CONV_K ** -0.5
    g_conv = 1.0 + 0.02 * jax.random.normal(ks[3], (DEPTH, CONV_WIDTH), f32)
    g_attn = 1.0 + 0.02 * jax.random.normal(ks[4], (DEPTH, ATTN_WIDTH), f32)
    w_out = jax.random.normal(ks[5], (DEPTH, MIX_WIDTH, D_MODEL), f32) * (MIX_WIDTH ** -0.5) * BETA
    ln1_g = 1.0 + 0.02 * jax.random.normal(ks[6], (DEPTH, D_MODEL), f32)
    ln1_b = 0.02 * jax.random.normal(ks[7], (DEPTH, D_MODEL), f32)
    w_up = jax.random.normal(ks[8], (DEPTH, D_MODEL, D_FF), f32) * D_MODEL ** -0.5
    w_down = jax.random.normal(ks[9], (DEPTH, D_FF, D_MODEL), f32) * (D_FF ** -0.5) * BETA
    ln2_g = 1.0 + 0.02 * jax.random.normal(ks[10], (DEPTH, D_MODEL), f32)
    ln2_b = 0.02 * jax.random.normal(ks[11], (DEPTH, D_MODEL), f32)
    return {"x": x, "w_in": w_in, "conv_w": conv_w, "g_conv": g_conv,
            "g_attn": g_attn, "w_out": w_out, "ln1_g": ln1_g, "ln1_b": ln1_b,
            "w_up": w_up, "w_down": w_down, "ln2_g": ln2_g, "ln2_b": ln2_b}


def reference(x, w_in, conv_w, g_conv, g_attn, w_out, ln1_g, ln1_b,
              w_up, w_down, ln2_g, ln2_b):
    bsz, seq, _ = x.shape
    split_at = [CONV_WIDTH, 2 * CONV_WIDTH, 3 * CONV_WIDTH,
                3 * CONV_WIDTH + ATTN_WIDTH, 3 * CONV_WIDTH + 2 * ATTN_WIDTH]
    for l in range(DEPTH):
        proj = jnp.einsum('bsd,dc->bsc', x, w_in[l])
        b_gate, c_gate, h_conv, q, k, v = jnp.split(proj, split_at, axis=-1)
        y_conv = short_gated_conv(b_gate, c_gate, h_conv, conv_w[l])
        qh = q.reshape(bsz, seq, ATTN_HEADS, HEAD_DIM)
        kh = k.reshape(bsz, seq, ATTN_HEADS, HEAD_DIM)
        vh = v.reshape(bsz, seq, ATTN_HEADS, HEAD_DIM)
        y_attn = stick_breaking_attention(qh, kh, vh)
        y_mix = jnp.concatenate([group_rmsnorm(y_conv, g_conv[l], CONV_GROUPS),
                                 group_rmsnorm(y_attn, g_attn[l], ATTN_HEADS)], axis=-1)
        mix_out = jnp.einsum('bsc,cd->bsd', y_mix, w_out[l])
        x = layer_norm(ALPHA * x + mix_out, ln1_g[l], ln1_b[l])
        hid = jnp.square(jax.nn.relu(jnp.einsum('bsd,df->bsf', x, w_up[l])))
        ffn_out = jnp.einsum('bsf,fd->bsd', hid, w_down[l])
        x = layer_norm(ALPHA * x + ffn_out, ln2_g[l], ln2_b[l])
    return x
```

```python
import functools

import jax
import jax.numpy as jnp
from jax import lax
from jax.experimental import pallas as pl
from jax.experimental.pallas import tpu as pltpu

F32 = jnp.float32
BF16 = jnp.bfloat16

CONV_GROUPS = 8
CONV_K = 3
ATTN_HEADS = 8
LN_EPS = 1e-5
RMS_EPS = 1e-6

LANES = 128
VMEM_LIMIT_BYTES = 56 * 1024 * 1024

PROJ_ROWS = 256
ATTN_BLOCK = 128
MLP_ROWS = 256
MLP_CHUNK = 1024


def _split_bf16(a):
    hi = a.astype(BF16)
    lo = (a - hi.astype(F32)).astype(BF16)
    return hi, lo


def _dot(a, b):
    return jnp.dot(a, b, preferred_element_type=F32)


def _dot_split(a, b):
    hi, lo = _split_bf16(a)
    return _dot(hi, b) + _dot(lo, b)


def _proj_kernel(x_ref, w_ref, cw_ref, g_ref, gm_ref,
                 yc_ref, q_ref, k_ref, v_ref, u_ref, *, rows, cw, aw, q_scale):
    i = pl.program_id(1)
    xb = x_ref[...].astype(BF16)

    def proj(c0, width):
        return _dot(xb, w_ref[:, c0:c0 + width])

    q_ref[...] = (proj(3 * cw, aw) * q_scale).astype(BF16)
    k_ref[...] = proj(3 * cw + aw, aw).astype(BF16)
    v_ref[...] = proj(3 * cw + 2 * aw, aw).astype(BF16)

    b_gate = proj(0, cw)
    u = proj(cw, cw) * proj(2 * cw, cw)

    @pl.when(i == 0)
    def _():
        u_ref[0:8, :] = jnp.zeros((8, cw), F32)

    @pl.when(i > 0)
    def _():
        u_ref[0:8, :] = u_ref[rows:rows + 8, :]

    u_ref[8:rows + 8, :] = u
    u1 = u_ref[7:rows + 7, :]
    u2 = u_ref[6:rows + 6, :]
    y = b_gate * (cw_ref[0:1, :] * u2 + cw_ref[1:2, :] * u1 + cw_ref[2:3, :] * u)

    ms = _dot_split(y * y, gm_ref[...])
    yc_ref[...] = (y * lax.rsqrt(ms + RMS_EPS) * g_ref[...]).astype(BF16)


def _group_mean_matrix(width, groups):
    gid = jnp.arange(width) // (width // groups)
    return jnp.where(gid[:, None] == gid[None, :], groups / width, 0.0).astype(BF16)


def _proj_call(x, w_in, conv_w, g_conv, cw, aw):
    bsz, seq, d = x.shape
    rows = PROJ_ROWS
    head_dim = aw // ATTN_HEADS
    kern = functools.partial(_proj_kernel, rows=rows, cw=cw, aw=aw,
                             q_scale=head_dim ** -0.5)
    out_spec = pl.BlockSpec((None, rows, cw), lambda b, i: (b, i, 0))
    out_spec_a = pl.BlockSpec((None, rows, aw), lambda b, i: (b, i, 0))
    return pl.pallas_call(
        kern,
        grid=(bsz, seq // rows),
        in_specs=[
            pl.BlockSpec((None, rows, d), lambda b, i: (b, i, 0)),
            pl.BlockSpec(w_in.shape, lambda b, i: (0, 0)),
            pl.BlockSpec(conv_w.shape, lambda b, i: (0, 0)),
            pl.BlockSpec((1, cw), lambda b, i: (0, 0)),
            pl.BlockSpec((cw, cw), lambda b, i: (0, 0)),
        ],
        out_specs=[out_spec, out_spec_a, out_spec_a, out_spec_a],
        out_shape=[jax.ShapeDtypeStruct((bsz, seq, cw), BF16)]
        + [jax.ShapeDtypeStruct((bsz, seq, aw), BF16)] * 3,
        scratch_shapes=[pltpu.VMEM((rows + 8, cw), F32)],
        compiler_params=pltpu.CompilerParams(
            dimension_semantics=("arbitrary", "arbitrary"),
            vmem_limit_bytes=VMEM_LIMIT_BYTES),
        name="proj_conv",
    )(x, w_in.astype(BF16), conv_w, g_conv.reshape(1, cw),
      _group_mean_matrix(cw, CONV_GROUPS))


def _attn_kernel(q_ref, k_ref, v_ref, uo_ref, gm_ref, g_ref, o_ref, *, blk, head_dim):
    i = pl.program_id(2)
    lane = lax.broadcasted_iota(jnp.int32, (blk, LANES), 1)
    row = lax.broadcasted_iota(jnp.int32, (blk, blk), 0)
    col = lax.broadcasted_iota(jnp.int32, (blk, blk), 1)
    causal = col < row
    q = q_ref[...]
    uo = uo_ref[...]

    def block(qm, kj, vj, carry, acc, mask):
        z = lax.dot_general(qm, kj, (((1,), (1,)), ((), ())),
                            preferred_element_type=F32)
        soft = jnp.log(1.0 + jnp.exp(-jnp.abs(z)))
        lk = -jnp.maximum(z, 0.0) - soft
        lb = jnp.minimum(z, 0.0) - soft
        if mask is not None:
            lk = jnp.where(mask, lk, 0.0)
        r = _dot_split(lk, uo)
        a = jnp.exp(lb + r[:, :blk] + carry)
        if mask is not None:
            a = jnp.where(mask, a, 0.0)
        acc = acc + _dot(a.astype(BF16), vj)
        return carry + r[:, blk:], acc

    outs = []
    for h in range(LANES // head_dim):
        in_head = (lane >= h * head_dim) & (lane < (h + 1) * head_dim)
        qm = jnp.where(in_head, q, jnp.zeros_like(q))
        r0 = pl.multiple_of(i * blk, blk)
        carry, acc = block(qm, k_ref[pl.ds(r0, blk), :], v_ref[pl.ds(r0, blk), :],
                           jnp.zeros((blk, blk), F32), jnp.zeros((blk, LANES), F32),
                           causal)

        def body(t, c, qm=qm):
            rj = pl.multiple_of((i - 1 - t) * blk, blk)
            return block(qm, k_ref[pl.ds(rj, blk), :], v_ref[pl.ds(rj, blk), :],
                         c[0], c[1], None)

        carry, acc = lax.fori_loop(0, i, body, (carry, acc))
        outs.append(jnp.where(in_head, acc, 0.0))

    o = outs[0]
    for extra in outs[1:]:
        o = o + extra
    ms = _dot_split(o * o, gm_ref[...])
    o_ref[...] = (o * lax.rsqrt(ms + RMS_EPS) * g_ref[...]).astype(BF16)


def _attn_call(q, k, v, g_attn):
    bsz, seq, aw = q.shape
    blk = ATTN_BLOCK
    head_dim = aw // ATTN_HEADS
    kern = functools.partial(_attn_kernel, blk=blk, head_dim=head_dim)
    jj = jnp.arange(blk)
    uo = jnp.concatenate([(jj[:, None] > jj[None, :]).astype(BF16),
                          jnp.ones((blk, blk), BF16)], axis=1)
    kv_spec = pl.BlockSpec((None, seq, LANES), lambda b, p, i: (b, 0, p))
    return pl.pallas_call(
        kern,
        grid=(bsz, aw // LANES, seq // blk),
        in_specs=[
            pl.BlockSpec((None, blk, LANES), lambda b, p, i: (b, i, p)),
            kv_spec, kv_spec,
            pl.BlockSpec((blk, 2 * blk), lambda b, p, i: (0, 0)),
            pl.BlockSpec((LANES, LANES), lambda b, p, i: (0, 0)),
            pl.BlockSpec((1, LANES), lambda b, p, i: (0, p)),
        ],
        out_specs=pl.BlockSpec((None, blk, LANES), lambda b, p, i: (b, i, p)),
        out_shape=jax.ShapeDtypeStruct((bsz, seq, aw), BF16),
        compiler_params=pltpu.CompilerParams(
            dimension_semantics=("arbitrary", "arbitrary", "arbitrary"),
            vmem_limit_bytes=VMEM_LIMIT_BYTES),
        name="stickbreak_attn",
    )(q, k, v, uo, _group_mean_matrix(LANES, LANES // head_dim),
      g_attn.reshape(1, aw))


def _layer_norm(x, g, b):
    mu = jnp.mean(x, axis=-1, keepdims=True)
    xc = x - mu
    var = jnp.mean(xc * xc, axis=-1, keepdims=True)
    return xc * lax.rsqrt(var + LN_EPS) * g + b


def _mlp_kernel(x_ref, yc_ref, ya_ref, wo_ref, g1_ref, b1_ref, wu_ref, wd_ref,
                g2_ref, b2_ref, o_ref, *, cw, alpha, chunk):
    mix = _dot(yc_ref[...], wo_ref[0:cw, :]) + _dot(ya_ref[...], wo_ref[cw:, :])
    x1 = _layer_norm(alpha * x_ref[...] + mix, g1_ref[...], b1_ref[...])
    x1b = x1.astype(BF16)
    ffn = None
    for c0 in range(0, wu_ref.shape[1], chunk):
        hid = jnp.maximum(_dot(x1b, wu_ref[:, c0:c0 + chunk]), 0.0)
        part = _dot((hid * hid).astype(BF16), wd_ref[c0:c0 + chunk, :])
        ffn = part if ffn is None else ffn + part
    o_ref[...] = _layer_norm(alpha * x1 + ffn, g2_ref[...], b2_ref[...])


def _mlp_call(x2, yc2, ya2, w_out, ln1_g, ln1_b, w_up, w_down, ln2_g, ln2_b, alpha):
    t, d = x2.shape
    cw = yc2.shape[1]
    aw = ya2.shape[1]
    rows = MLP_ROWS
    kern = functools.partial(_mlp_kernel, cw=cw, alpha=alpha, chunk=MLP_CHUNK)

    def const(shape):
        return pl.BlockSpec(shape, lambda i: (0, 0), pipeline_mode=pl.Buffered(1))

    return pl.pallas_call(
        kern,
        grid=(t // rows,),
        in_specs=[
            pl.BlockSpec((rows, d), lambda i: (i, 0)),
            pl.BlockSpec((rows, cw), lambda i: (i, 0)),
            pl.BlockSpec((rows, aw), lambda i: (i, 0)),
            const(w_out.shape), const((1, d)), const((1, d)),
            const(w_up.shape), const(w_down.shape), const((1, d)), const((1, d)),
        ],
        out_specs=pl.BlockSpec((rows, d), lambda i: (i, 0)),
        out_shape=jax.ShapeDtypeStruct((t, d), F32),
        compiler_params=pltpu.CompilerParams(
            dimension_semantics=("arbitrary",),
            vmem_limit_bytes=VMEM_LIMIT_BYTES),
        name="outproj_mlp",
    )(x2, yc2, ya2, w_out.astype(BF16), ln1_g.reshape(1, d), ln1_b.reshape(1, d),
      w_up.astype(BF16), w_down.astype(BF16), ln2_g.reshape(1, d), ln2_b.reshape(1, d))


def kernel(x, w_in, conv_w, g_conv, g_attn, w_out, ln1_g, ln1_b, w_up, w_down, ln2_g, ln2_b):
    bsz, seq, d = x.shape
    depth = w_in.shape[0]
    cw = conv_w.shape[2]
    aw = g_attn.shape[1]
    assert w_in.shape[2] == 3 * cw + 3 * aw and w_out.shape[1] == cw + aw
    assert aw % LANES == 0 and seq % PROJ_ROWS == 0 and seq % ATTN_BLOCK == 0
    alpha = float((2 * depth) ** 0.25)
    for l in range(depth):
        yc, q, k, v = _proj_call(x, w_in[l], conv_w[l], g_conv[l], cw, aw)
        ya = _attn_call(q, k, v, g_attn[l:l + 1])
        x = _mlp_call(x.reshape(bsz * seq, d), yc.reshape(bsz * seq, cw),
                      ya.reshape(bsz * seq, aw), w_out[l], ln1_g[l], ln1_b[l],
                      w_up[l], w_down[l], ln2_g[l], ln2_b[l], alpha
                      ).reshape(bsz, seq, d)
    return x
```

```python
import functools

import jax
import jax.numpy as jnp
from jax import lax
from jax.experimental import pallas as pl
from jax.experimental.pallas import tpu as pltpu

F32 = jnp.float32
BF16 = jnp.bfloat16

CONV_GROUPS = 8
CONV_K = 3
ATTN_HEADS = 8
LN_EPS = 1e-5
RMS_EPS = 1e-6
SOFTPLUS_CLAMP = 40.0

LANES = 128
VMEM_LIMIT_BYTES = 56 * 1024 * 1024

PROJ_ROWS = 256
ATTN_TILE = 256
ATTN_ROWS = 512
MLP_ROWS = 256
MLP_CHUNK = 1024


def _split_bf16(a):
    hi = a.astype(BF16)
    lo = (a - hi.astype(F32)).astype(BF16)
    return hi, lo


def _dot(a, b):
    return jnp.dot(a, b, preferred_element_type=F32)


def _dot_split(a, b):
    hi, lo = _split_bf16(a)
    return _dot(hi, b) + _dot(lo, b)


def _proj_kernel(x_ref, w_ref, cw_ref, g_ref, gm_ref,
                 yc_ref, q_ref, k_ref, v_ref, u_ref, *, rows, cw, aw, q_scale):
    i = pl.program_id(1)
    xb = x_ref[...].astype(BF16)

    def proj(c0, width):
        return _dot(xb, w_ref[:, c0:c0 + width])

    q_ref[...] = (proj(3 * cw, aw) * q_scale).astype(BF16)
    k_ref[...] = proj(3 * cw + aw, aw).astype(BF16)
    v_ref[...] = proj(3 * cw + 2 * aw, aw).astype(BF16)

    b_gate = proj(0, cw)
    u = proj(cw, cw) * proj(2 * cw, cw)

    @pl.when(i == 0)
    def _():
        u_ref[0:8, :] = jnp.zeros((8, cw), F32)

    @pl.when(i > 0)
    def _():
        u_ref[0:8, :] = u_ref[rows:rows + 8, :]

    u_ref[8:rows + 8, :] = u
    u1 = u_ref[7:rows + 7, :]
    u2 = u_ref[6:rows + 6, :]
    y = b_gate * (cw_ref[0:1, :] * u2 + cw_ref[1:2, :] * u1 + cw_ref[2:3, :] * u)

    ms = _dot_split(y * y, gm_ref[...])
    yc_ref[...] = (y * lax.rsqrt(ms + RMS_EPS) * g_ref[...]).astype(BF16)


def _group_mean_matrix(width, groups):
    gid = jnp.arange(width) // (width // groups)
    return jnp.where(gid[:, None] == gid[None, :], groups / width, 0.0).astype(BF16)


def _proj_call(x, w_in, conv_w, g_conv, cw, aw):
    bsz, seq, d = x.shape
    rows = PROJ_ROWS
    head_dim = aw // ATTN_HEADS
    kern = functools.partial(_proj_kernel, rows=rows, cw=cw, aw=aw,
                             q_scale=head_dim ** -0.5)
    out_spec = pl.BlockSpec((None, rows, cw), lambda b, i: (b, i, 0))
    out_spec_a = pl.BlockSpec((None, rows, aw), lambda b, i: (b, i, 0))
    return pl.pallas_call(
        kern,
        grid=(bsz, seq // rows),
        in_specs=[
            pl.BlockSpec((None, rows, d), lambda b, i: (b, i, 0)),
            pl.BlockSpec(w_in.shape, lambda b, i: (0, 0)),
            pl.BlockSpec(conv_w.shape, lambda b, i: (0, 0)),
            pl.BlockSpec((1, cw), lambda b, i: (0, 0)),
            pl.BlockSpec((cw, cw), lambda b, i: (0, 0)),
        ],
        out_specs=[out_spec, out_spec_a, out_spec_a, out_spec_a],
        out_shape=[jax.ShapeDtypeStruct((bsz, seq, cw), BF16)]
        + [jax.ShapeDtypeStruct((bsz, seq, aw), BF16)] * 3,
        scratch_shapes=[pltpu.VMEM((rows + 8, cw), F32)],
        compiler_params=pltpu.CompilerParams(
            dimension_semantics=("arbitrary", "arbitrary"),
            vmem_limit_bytes=VMEM_LIMIT_BYTES),
        name="proj_conv",
    )(x, w_in.astype(BF16), conv_w, g_conv.reshape(1, cw),
      _group_mean_matrix(cw, CONV_GROUPS))


def _attn_kernel(q_ref, k_ref, v_ref, u_ref, gm_ref, g_ref, o_ref,
                 qm_ref, carry_ref, acc_ref, *, tile, head_dim):
    i = pl.program_id(2)
    heads = LANES // head_dim
    sub = ATTN_ROWS
    lane = lax.broadcasted_iota(jnp.int32, (tile, LANES), 1)
    in_head = [(lane >= h * head_dim) & (lane < (h + 1) * head_dim)
               for h in range(heads)]
    q = q_ref[...]
    for h in range(heads):
        qm_ref[h * tile:(h + 1) * tile, :] = jnp.where(in_head[h], q, jnp.zeros_like(q))
    carry_ref[...] = jnp.zeros_like(carry_ref)
    acc_ref[...] = jnp.zeros_like(acc_ref)

    def key_tile(j, diagonal):
        r0 = pl.multiple_of(j * tile, tile)
        kj = k_ref[pl.ds(r0, tile), :]
        vj = v_ref[pl.ds(r0, tile), :]
        for g in range(heads * tile // sub):
            rows = slice(g * sub, (g + 1) * sub)
            z = lax.dot_general(qm_ref[rows, :], kj, (((1,), (1,)), ((), ())),
                                preferred_element_type=F32)
            sp = jnp.maximum(z, jnp.log(1.0 + jnp.exp(jnp.minimum(z, SOFTPLUS_CLAMP))))
            lb = z - sp
            if diagonal:
                qrow = (g * sub + lax.broadcasted_iota(jnp.int32, (sub, tile), 0)) & (tile - 1)
                mask = lax.broadcasted_iota(jnp.int32, (sub, tile), 1) < qrow
                sp = jnp.where(mask, sp, 0.0)
            hi, lo = _split_bf16(sp)
            suffix = _dot(jnp.concatenate([hi, lo], axis=1), u_ref[...])
            total = jnp.sum(sp, axis=-1, keepdims=True)
            carry = carry_ref[rows, :]
            a = jnp.exp(lb - suffix - jnp.concatenate([carry] * (tile // LANES), axis=1))
            if diagonal:
                a = jnp.where(mask, a, 0.0)
            acc_ref[rows, :] += _dot(a.astype(BF16), vj)
            carry_ref[rows, :] = carry + total

    key_tile(i, True)

    def body(t, c):
        key_tile(i - 1 - t, False)
        return c

    lax.fori_loop(0, i, body, 0)

    o = jnp.where(in_head[0], acc_ref[0:tile, :], 0.0)
    for h in range(1, heads):
        o = o + jnp.where(in_head[h], acc_ref[h * tile:(h + 1) * tile, :], 0.0)
    ms = _dot_split(o * o, gm_ref[...])
    o_ref[...] = (o * lax.rsqrt(ms + RMS_EPS) * g_ref[...]).astype(BF16)


def _attn_call(q, k, v, g_attn):
    bsz, seq, aw = q.shape
    tile = ATTN_TILE
    head_dim = aw // ATTN_HEADS
    heads = LANES // head_dim
    kern = functools.partial(_attn_kernel, tile=tile, head_dim=head_dim)
    jj = jnp.arange(tile)
    u = (jj[:, None] > jj[None, :]).astype(BF16)
    u = jnp.concatenate([u, u], axis=0)
    kv_spec = pl.BlockSpec((None, seq, LANES), lambda b, p, i: (b, 0, p))
    return pl.pallas_call(
        kern,
        grid=(bsz, aw // LANES, seq // tile),
        in_specs=[
            pl.BlockSpec((None, tile, LANES), lambda b, p, i: (b, i, p)),
            kv_spec, kv_spec,
            pl.BlockSpec((2 * tile, tile), lambda b, p, i: (0, 0)),
            pl.BlockSpec((LANES, LANES), lambda b, p, i: (0, 0)),
            pl.BlockSpec((1, LANES), lambda b, p, i: (0, p)),
        ],
        out_specs=pl.BlockSpec((None, tile, LANES), lambda b, p, i: (b, i, p)),
        out_shape=jax.ShapeDtypeStruct((bsz, seq, aw), BF16),
        scratch_shapes=[pltpu.VMEM((heads * tile, LANES), BF16),
                        pltpu.VMEM((heads * tile, LANES), F32),
                        pltpu.VMEM((heads * tile, LANES), F32)],
        compiler_params=pltpu.CompilerParams(
            dimension_semantics=("arbitrary", "arbitrary", "arbitrary"),
            vmem_limit_bytes=VMEM_LIMIT_BYTES),
        name="stickbreak_attn",
    )(q, k, v, u, _group_mean_matrix(LANES, heads), g_attn.reshape(1, aw))


def _layer_norm(x, g, b):
    mu = jnp.mean(x, axis=-1, keepdims=True)
    xc = x - mu
    var = jnp.mean(xc * xc, axis=-1, keepdims=True)
    return xc * lax.rsqrt(var + LN_EPS) * g + b


def _mlp_kernel(x_ref, yc_ref, ya_ref, wo_ref, g1_ref, b1_ref, wu_ref, wd_ref,
                g2_ref, b2_ref, o_ref, *, cw, alpha, chunk):
    mix = _dot(yc_ref[...], wo_ref[0:cw, :]) + _dot(ya_ref[...], wo_ref[cw:, :])
    x1 = _layer_norm(alpha * x_ref[...] + mix, g1_ref[...], b1_ref[...])
    x1b = x1.astype(BF16)
    ffn = None
    for c0 in range(0, wu_ref.shape[1], chunk):
        hid = jnp.maximum(_dot(x1b, wu_ref[:, c0:c0 + chunk]), 0.0)
        part = _dot((hid * hid).astype(BF16), wd_ref[c0:c0 + chunk, :])
        ffn = part if ffn is None else ffn + part
    o_ref[...] = _layer_norm(alpha * x1 + ffn, g2_ref[...], b2_ref[...])


def _mlp_call(x2, yc2, ya2, w_out, ln1_g, ln1_b, w_up, w_down, ln2_g, ln2_b, alpha):
    t, d = x2.shape
    cw = yc2.shape[1]
    aw = ya2.shape[1]
    rows = MLP_ROWS
    kern = functools.partial(_mlp_kernel, cw=cw, alpha=alpha, chunk=MLP_CHUNK)

    def const(shape):
        return pl.BlockSpec(shape, lambda i: (0, 0), pipeline_mode=pl.Buffered(1))

    return pl.pallas_call(
        kern,
        grid=(t // rows,),
        in_specs=[
            pl.BlockSpec((rows, d), lambda i: (i, 0)),
            pl.BlockSpec((rows, cw), lambda i: (i, 0)),
            pl.BlockSpec((rows, aw), lambda i: (i, 0)),
            const(w_out.shape), const((1, d)), const((1, d)),
            const(w_up.shape), const(w_down.shape), const((1, d)), const((1, d)),
        ],
        out_specs=pl.BlockSpec((rows, d), lambda i: (i, 0)),
        out_shape=jax.ShapeDtypeStruct((t, d), F32),
        compiler_params=pltpu.CompilerParams(
            dimension_semantics=("arbitrary",),
            vmem_limit_bytes=VMEM_LIMIT_BYTES),
        name="outproj_mlp",
    )(x2, yc2, ya2, w_out.astype(BF16), ln1_g.reshape(1, d), ln1_b.reshape(1, d),
      w_up.astype(BF16), w_down.astype(BF16), ln2_g.reshape(1, d), ln2_b.reshape(1, d))


def kernel(x, w_in, conv_w, g_conv, g_attn, w_out, ln1_g, ln1_b, w_up, w_down, ln2_g, ln2_b):
    bsz, seq, d = x.shape
    depth = w_in.shape[0]
    cw = conv_w.shape[2]
    aw = g_attn.shape[1]
    assert w_in.shape[2] == 3 * cw + 3 * aw and w_out.shape[1] == cw + aw
    assert aw % LANES == 0 and seq % PROJ_ROWS == 0 and seq % ATTN_TILE == 0
    assert ATTN_TILE & (ATTN_TILE - 1) == 0
    alpha = float((2 * depth) ** 0.25)
    for l in range(depth):
        yc, q, k, v = _proj_call(x, w_in[l], conv_w[l], g_conv[l], cw, aw)
        ya = _attn_call(q, k, v, g_attn[l:l + 1])
        x = _mlp_call(x.reshape(bsz * seq, d), yc.reshape(bsz * seq, cw),
                      ya.reshape(bsz * seq, aw), w_out[l], ln1_g[l], ln1_b[l],
                      w_up[l], w_down[l], ln2_g[l], ln2_b[l], alpha
                      ).reshape(bsz, seq, d)
    return x
```

```python
import functools

import jax
import jax.numpy as jnp
from jax import lax
from jax.experimental import pallas as pl
from jax.experimental.pallas import tpu as pltpu

F32 = jnp.float32
BF16 = jnp.bfloat16

CONV_GROUPS = 8
CONV_K = 3
ATTN_HEADS = 8
LN_EPS = 1e-5
RMS_EPS = 1e-6
SOFTPLUS_CLAMP = 40.0
MASKED_LOGIT = -1e4
NO_MASK = 3e38

LANES = 128
VMEM_LIMIT_BYTES = 56 * 1024 * 1024

PROJ_ROWS = 256
ATTN_TILE = 256
ATTN_UNROLL = 8
NORM_ROWS = 512
MLP_ROWS = 256
MLP_CHUNK = 1024


def _split_bf16(a):
    hi = a.astype(BF16)
    lo = (a - hi.astype(F32)).astype(BF16)
    return hi, lo


def _dot(a, b):
    return jnp.dot(a, b, preferred_element_type=F32)


def _dot_split(a, b):
    hi, lo = _split_bf16(a)
    return _dot(hi, b) + _dot(lo, b)


def _proj_kernel(x_ref, w_ref, cw_ref, g_ref, gm_ref,
                 yc_ref, q_ref, k_ref, v_ref, u_ref, *, rows, cw, aw, q_scale):
    i = pl.program_id(1)
    xb = x_ref[...].astype(BF16)

    def proj(c0, width):
        return _dot(xb, w_ref[:, c0:c0 + width])

    q_ref[...] = (proj(3 * cw, aw) * q_scale).astype(BF16)
    k_ref[...] = proj(3 * cw + aw, aw).astype(BF16)
    v_ref[...] = proj(3 * cw + 2 * aw, aw).astype(BF16)

    b_gate = proj(0, cw)
    u = proj(cw, cw) * proj(2 * cw, cw)

    @pl.when(i == 0)
    def _():
        u_ref[0:8, :] = jnp.zeros((8, cw), F32)

    @pl.when(i > 0)
    def _():
        u_ref[0:8, :] = u_ref[rows:rows + 8, :]

    u_ref[8:rows + 8, :] = u
    u1 = u_ref[7:rows + 7, :]
    u2 = u_ref[6:rows + 6, :]
    y = b_gate * (cw_ref[0:1, :] * u2 + cw_ref[1:2, :] * u1 + cw_ref[2:3, :] * u)

    ms = _dot_split(y * y, gm_ref[...])
    yc_ref[...] = (y * lax.rsqrt(ms + RMS_EPS) * g_ref[...]).astype(BF16)


def _group_mean_matrix(width, groups):
    gid = jnp.arange(width) // (width // groups)
    return jnp.where(gid[:, None] == gid[None, :], groups / width, 0.0).astype(BF16)


def _proj_call(x, w_in, conv_w, g_conv, cw, aw):
    bsz, seq, d = x.shape
    rows = PROJ_ROWS
    head_dim = aw // ATTN_HEADS
    kern = functools.partial(_proj_kernel, rows=rows, cw=cw, aw=aw,
                             q_scale=head_dim ** -0.5)
    out_spec = pl.BlockSpec((None, rows, cw), lambda b, i: (b, i, 0))
    out_spec_a = pl.BlockSpec((None, rows, aw), lambda b, i: (b, i, 0))
    return pl.pallas_call(
        kern,
        grid=(bsz, seq // rows),
        in_specs=[
            pl.BlockSpec((None, rows, d), lambda b, i: (b, i, 0)),
            pl.BlockSpec(w_in.shape, lambda b, i: (0, 0)),
            pl.BlockSpec(conv_w.shape, lambda b, i: (0, 0)),
            pl.BlockSpec((1, cw), lambda b, i: (0, 0)),
            pl.BlockSpec((cw, cw), lambda b, i: (0, 0)),
        ],
        out_specs=[out_spec, out_spec_a, out_spec_a, out_spec_a],
        out_shape=[jax.ShapeDtypeStruct((bsz, seq, cw), BF16)]
        + [jax.ShapeDtypeStruct((bsz, seq, aw), BF16)] * 3,
        scratch_shapes=[pltpu.VMEM((rows + 8, cw), F32)],
        compiler_params=pltpu.CompilerParams(
            dimension_semantics=("arbitrary", "arbitrary"),
            vmem_limit_bytes=VMEM_LIMIT_BYTES),
        name="proj_conv",
    )(x, w_in.astype(BF16), conv_w, g_conv.reshape(1, cw),
      _group_mean_matrix(cw, CONV_GROUPS))


def _attn_kernel(tq_ref, tk_ref, q_ref, k_ref, v_ref, u_ref, cm_ref, gm_ref, g_ref, o_ref,
                 qm_ref, z_ref, hl_ref, lbc_ref, a_ref, carry_ref, acc_ref, oraw_ref,
                 *, tile, head_dim, n_steps):
    heads = LANES // head_dim
    seq = q_ref.shape[0]
    lane = lax.broadcasted_iota(jnp.int32, (1, LANES), 1)
    in_head = [(lane >= h * head_dim) & (lane < (h + 1) * head_dim)
               for h in range(heads)]
    q = q_ref[...]
    for h in range(heads):
        qm_ref[h] = jnp.where(in_head[h], q, jnp.zeros_like(q))
    carry_ref[...] = jnp.zeros_like(carry_ref)
    acc_ref[...] = jnp.zeros_like(acc_ref)
    a_ref[...] = jnp.zeros_like(a_ref)
    hl_ref[1] = jnp.zeros(hl_ref.shape[1:], BF16)
    lbc_ref[1] = jnp.full(lbc_ref.shape[1:], MASKED_LOGIT, F32)
    z_ref[1] = jnp.full(z_ref.shape[1:], MASKED_LOGIT, F32)

    def stage0(s, p):
        q0 = tq_ref[s + 3]
        k0 = tk_ref[s + 3]
        r0 = pl.multiple_of(q0 * tile, tile)
        qm = jnp.concatenate([qm_ref[h, pl.ds(r0, tile), :] for h in range(heads)], axis=0)
        kt = k_ref[pl.ds(pl.multiple_of(k0 * tile, tile), tile), :]
        z = lax.dot_general(qm, kt, (((1,), (1,)), ((), ())), preferred_element_type=F32)
        z_ref[p] = jnp.minimum(z, cm_ref[(q0 == k0).astype(jnp.int32)])

    def stage1(s, p):
        diag = tq_ref[s + 2] == tk_ref[s + 2]
        z = z_ref[1 - p]
        sp = jnp.maximum(z, jnp.log(1.0 + jnp.exp(jnp.minimum(z, SOFTPLUS_CLAMP))))
        carry = jnp.where(diag, 0.0, carry_ref[...])
        lbc_ref[p] = z - jnp.concatenate([carry] * (tile // LANES), axis=1)
        hi, lo = _split_bf16(sp)
        hl_ref[p] = jnp.concatenate([hi, lo], axis=1)
        carry_ref[...] = carry + jnp.sum(sp, axis=-1, keepdims=True)

    def stage2(p):
        suffix = _dot(hl_ref[1 - p], u_ref[...])
        a_ref[1 - p] = jnp.exp(lbc_ref[1 - p] - suffix).astype(BF16)

    def stage3(s, p):
        q3 = tq_ref[s]
        k3 = tk_ref[s]
        vt = v_ref[pl.ds(pl.multiple_of(k3 * tile, tile), tile), :]
        acc = jnp.where(q3 == k3, 0.0, acc_ref[...]) + _dot(a_ref[p], vt)
        acc_ref[...] = acc
        r3 = pl.multiple_of(q3 * tile, tile)
        for h in range(heads):
            oraw_ref[h, pl.ds(r3, tile), :] = acc[h * tile:(h + 1) * tile]

    def step(s, p):
        stage1(s, p)
        stage2(p)
        stage3(s, p)
        stage0(s, p)

    def body(m, c):
        for j in range(ATTN_UNROLL):
            step(ATTN_UNROLL * m + j, j % 2)
        return c

    lax.fori_loop(0, n_steps // ATTN_UNROLL, body, 0)

    def normalise(t, c):
        r = pl.multiple_of(t * NORM_ROWS, NORM_ROWS)
        o = jnp.where(in_head[0], oraw_ref[0, pl.ds(r, NORM_ROWS), :], 0.0)
        for h in range(1, heads):
            o = o + jnp.where(in_head[h], oraw_ref[h, pl.ds(r, NORM_ROWS), :], 0.0)
        ms = _dot_split(o * o, gm_ref[...])
        o_ref[pl.ds(r, NORM_ROWS), :] = (o * lax.rsqrt(ms + RMS_EPS) * g_ref[...]).astype(BF16)
        return c

    lax.fori_loop(0, seq // NORM_ROWS, normalise, 0)


def _attn_call(q, k, v, g_attn):
    bsz, seq, aw = q.shape
    tile = ATTN_TILE
    head_dim = aw // ATTN_HEADS
    heads = LANES // head_dim
    rows = heads * tile
    n_q = seq // tile
    tiles = [(i, j) for i in range(n_q) for j in range(i, -1, -1)]
    lead = 3 + (-(len(tiles) + 3)) % ATTN_UNROLL
    n_steps = lead + len(tiles)
    table = [(0, 0)] * lead + tiles + [(0, 0)] * 3
    tq = jnp.asarray([t[0] for t in table], jnp.int32)
    tk = jnp.asarray([t[1] for t in table], jnp.int32)
    kern = functools.partial(_attn_kernel, tile=tile, head_dim=head_dim, n_steps=n_steps)
    jj = jnp.arange(tile)
    u = (jj[:, None] >= jj[None, :]).astype(BF16)
    u = jnp.concatenate([u, u], axis=0)
    qpos = jnp.arange(rows)[:, None] % tile
    cm = jnp.stack([jnp.full((rows, tile), NO_MASK, F32),
                    jnp.where(jj[None, :] < qpos, NO_MASK, MASKED_LOGIT).astype(F32)])
    seq_spec = pl.BlockSpec((None, seq, LANES), lambda b, p, *_: (b, 0, p))

    def const(shape):
        return pl.BlockSpec(shape, lambda b, p, *_: (0,) * len(shape))

    return pl.pallas_call(
        kern,
        grid_spec=pltpu.PrefetchScalarGridSpec(
            num_scalar_prefetch=2,
            grid=(bsz, aw // LANES),
            in_specs=[seq_spec, seq_spec, seq_spec,
                      const((2 * tile, tile)), const((2, rows, tile)),
                      const((LANES, LANES)),
                      pl.BlockSpec((1, LANES), lambda b, p, *_: (0, p))],
            out_specs=seq_spec,
            scratch_shapes=[pltpu.VMEM((heads, seq, LANES), BF16),
                            pltpu.VMEM((2, rows, tile), F32),
                            pltpu.VMEM((2, rows, 2 * tile), BF16),
                            pltpu.VMEM((2, rows, tile), F32),
                            pltpu.VMEM((2, rows, tile), BF16),
                            pltpu.VMEM((rows, LANES), F32),
                            pltpu.VMEM((rows, LANES), F32),
                            pltpu.VMEM((heads, seq, LANES), F32)]),
        out_shape=jax.ShapeDtypeStruct((bsz, seq, aw), BF16),
        compiler_params=pltpu.CompilerParams(
            dimension_semantics=("arbitrary", "arbitrary"),
            vmem_limit_bytes=VMEM_LIMIT_BYTES),
        name="stickbreak_attn",
    )(tq, tk, q, k, v, u, cm, _group_mean_matrix(LANES, heads), g_attn.reshape(1, aw))


def _layer_norm(x, g, b):
    mu = jnp.mean(x, axis=-1, keepdims=True)
    xc = x - mu
    var = jnp.mean(xc * xc, axis=-1, keepdims=True)
    return xc * lax.rsqrt(var + LN_EPS) * g + b


def _mlp_kernel(x_ref, yc_ref, ya_ref, wo_ref, g1_ref, b1_ref, wu_ref, wd_ref,
                g2_ref, b2_ref, o_ref, *, cw, alpha, chunk):
    mix = _dot(yc_ref[...], wo_ref[0:cw, :]) + _dot(ya_ref[...], wo_ref[cw:, :])
    x1 = _layer_norm(alpha * x_ref[...] + mix, g1_ref[...], b1_ref[...])
    x1b = x1.astype(BF16)
    ffn = None
    for c0 in range(0, wu_ref.shape[1], chunk):
        hid = jnp.maximum(_dot(x1b, wu_ref[:, c0:c0 + chunk]), 0.0)
        part = _dot((hid * hid).astype(BF16), wd_ref[c0:c0 + chunk, :])
        ffn = part if ffn is None else ffn + part
    o_ref[...] = _layer_norm(alpha * x1 + ffn, g2_ref[...], b2_ref[...])


def _mlp_call(x2, yc2, ya2, w_out, ln1_g, ln1_b, w_up, w_down, ln2_g, ln2_b, alpha):
    t, d = x2.shape
    cw = yc2.shape[1]
    aw = ya2.shape[1]
    rows = MLP_ROWS
    kern = functools.partial(_mlp_kernel, cw=cw, alpha=alpha, chunk=MLP_CHUNK)

    def const(shape):
        return pl.BlockSpec(shape, lambda i: (0, 0), pipeline_mode=pl.Buffered(1))

    return pl.pallas_call(
        kern,
        grid=(t // rows,),
        in_specs=[
            pl.BlockSpec((rows, d), lambda i: (i, 0)),
            pl.BlockSpec((rows, cw), lambda i: (i, 0)),
            pl.BlockSpec((rows, aw), lambda i: (i, 0)),
            const(w_out.shape), const((1, d)), const((1, d)),
            const(w_up.shape), const(w_down.shape), const((1, d)), const((1, d)),
        ],
        out_specs=pl.BlockSpec((rows, d), lambda i: (i, 0)),
        out_shape=jax.ShapeDtypeStruct((t, d), F32),
        compiler_params=pltpu.CompilerParams(
            dimension_semantics=("arbitrary",),
            vmem_limit_bytes=VMEM_LIMIT_BYTES),
        name="outproj_mlp",
    )(x2, yc2, ya2, w_out.astype(BF16), ln1_g.reshape(1, d), ln1_b.reshape(1, d),
      w_up.astype(BF16), w_down.astype(BF16), ln2_g.reshape(1, d), ln2_b.reshape(1, d))


def kernel(x, w_in, conv_w, g_conv, g_attn, w_out, ln1_g, ln1_b, w_up, w_down, ln2_g, ln2_b):
    bsz, seq, d = x.shape
    depth = w_in.shape[0]
    cw = conv_w.shape[2]
    aw = g_attn.shape[1]
    assert w_in.shape[2] == 3 * cw + 3 * aw and w_out.shape[1] == cw + aw
    assert aw % LANES == 0 and seq % PROJ_ROWS == 0 and seq % NORM_ROWS == 0
    assert seq % ATTN_TILE == 0 and ATTN_TILE % LANES == 0
    alpha = float((2 * depth) ** 0.25)
    for l in range(depth):
        yc, q, k, v = _proj_call(x, w_in[l], conv_w[l], g_conv[l], cw, aw)
        ya = _attn_call(q, k, v, g_attn[l:l + 1])
        x = _mlp_call(x.reshape(bsz * seq, d), yc.reshape(bsz * seq, cw),
                      ya.reshape(bsz * seq, aw), w_out[l], ln1_g[l], ln1_b[l],
                      w_up[l], w_down[l], ln2_g[l], ln2_b[l], alpha
                      ).reshape(bsz, seq, d)
    return x
```

```python
import functools

import jax
import jax.numpy as jnp
from jax import lax
from jax.experimental import pallas as pl
from jax.experimental.pallas import tpu as pltpu

F32 = jnp.float32
BF16 = jnp.bfloat16

CONV_GROUPS = 8
CONV_K = 3
ATTN_HEADS = 8
LN_EPS = 1e-5
RMS_EPS = 1e-6
SOFTPLUS_CLAMP = 40.0
MASKED_LOGIT = -1e4
NO_MASK = 3e38
UNDERFLOW_MARGIN = 110.0
LOGIT_BOUND_SLACK = 1.01

LANES = 128
VMEM_LIMIT_BYTES = 56 * 1024 * 1024

PROJ_ROWS = 256
ATTN_TILE = 256
ATTN_UNROLL = 4
NORM_ROWS = 512
MLP_ROWS = 256
MLP_CHUNK = 1024


def _split_bf16(a):
    hi = a.astype(BF16)
    lo = (a - hi.astype(F32)).astype(BF16)
    return hi, lo


def _dot(a, b):
    return jnp.dot(a, b, preferred_element_type=F32)


def _dot_split(a, b):
    hi, lo = _split_bf16(a)
    return _dot(hi, b) + _dot(lo, b)


def _proj_kernel(x_ref, w_ref, cw_ref, g_ref, gm_ref,
                 yc_ref, q_ref, k_ref, v_ref, u_ref, *, rows, cw, aw, q_scale):
    i = pl.program_id(1)
    xb = x_ref[...].astype(BF16)

    def proj(c0, width):
        return _dot(xb, w_ref[:, c0:c0 + width])

    q_ref[...] = (proj(3 * cw, aw) * q_scale).astype(BF16)
    k_ref[...] = proj(3 * cw + aw, aw).astype(BF16)
    v_ref[...] = proj(3 * cw + 2 * aw, aw).astype(BF16)

    b_gate = proj(0, cw)
    u = proj(cw, cw) * proj(2 * cw, cw)

    @pl.when(i == 0)
    def _():
        u_ref[0:8, :] = jnp.zeros((8, cw), F32)

    @pl.when(i > 0)
    def _():
        u_ref[0:8, :] = u_ref[rows:rows + 8, :]

    u_ref[8:rows + 8, :] = u
    u1 = u_ref[7:rows + 7, :]
    u2 = u_ref[6:rows + 6, :]
    y = b_gate * (cw_ref[0:1, :] * u2 + cw_ref[1:2, :] * u1 + cw_ref[2:3, :] * u)

    ms = _dot_split(y * y, gm_ref[...])
    yc_ref[...] = (y * lax.rsqrt(ms + RMS_EPS) * g_ref[...]).astype(BF16)


def _group_mean_matrix(width, groups):
    gid = jnp.arange(width) // (width // groups)
    return jnp.where(gid[:, None] == gid[None, :], groups / width, 0.0).astype(BF16)


def _proj_call(x, w_in, conv_w, g_conv, cw, aw):
    bsz, seq, d = x.shape
    rows = PROJ_ROWS
    head_dim = aw // ATTN_HEADS
    kern = functools.partial(_proj_kernel, rows=rows, cw=cw, aw=aw,
                             q_scale=head_dim ** -0.5)
    out_spec = pl.BlockSpec((None, rows, cw), lambda b, i: (b, i, 0))
    out_spec_a = pl.BlockSpec((None, rows, aw), lambda b, i: (b, i, 0))
    return pl.pallas_call(
        kern,
        grid=(bsz, seq // rows),
        in_specs=[
            pl.BlockSpec((None, rows, d), lambda b, i: (b, i, 0)),
            pl.BlockSpec(w_in.shape, lambda b, i: (0, 0)),
            pl.BlockSpec(conv_w.shape, lambda b, i: (0, 0)),
            pl.BlockSpec((1, cw), lambda b, i: (0, 0)),
            pl.BlockSpec((cw, cw), lambda b, i: (0, 0)),
        ],
        out_specs=[out_spec, out_spec_a, out_spec_a, out_spec_a],
        out_shape=[jax.ShapeDtypeStruct((bsz, seq, cw), BF16)]
        + [jax.ShapeDtypeStruct((bsz, seq, aw), BF16)] * 3,
        scratch_shapes=[pltpu.VMEM((rows + 8, cw), F32)],
        compiler_params=pltpu.CompilerParams(
            dimension_semantics=("arbitrary", "arbitrary"),
            vmem_limit_bytes=VMEM_LIMIT_BYTES),
        name="proj_conv",
    )(x, w_in.astype(BF16), conv_w, g_conv.reshape(1, cw),
      _group_mean_matrix(cw, CONV_GROUPS))


def _attn_kernel(thr_ref, q_ref, k_ref, v_ref, u_ref, cm_ref, gm_ref, g_ref, o_ref,
                 qm_ref, z_ref, hl_ref, lbc_ref, a_ref, carry_ref, acc_ref, oraw_ref,
                 *, tile, head_dim):
    heads = LANES // head_dim
    seq = q_ref.shape[0]
    n_q = seq // tile
    thr = thr_ref[pl.program_id(0), pl.program_id(1)]
    lane = lax.broadcasted_iota(jnp.int32, (1, LANES), 1)
    in_head = [(lane >= h * head_dim) & (lane < (h + 1) * head_dim)
               for h in range(heads)]
    q = q_ref[...]
    for h in range(heads):
        qm_ref[h] = jnp.where(in_head[h], q, jnp.zeros_like(q))
    carry_ref[...] = jnp.zeros_like(carry_ref)
    acc_ref[...] = jnp.zeros_like(acc_ref)
    a_ref[...] = jnp.zeros_like(a_ref)
    hl_ref[1] = jnp.zeros(hl_ref.shape[1:], BF16)
    lbc_ref[1] = jnp.full(lbc_ref.shape[1:], MASKED_LOGIT, F32)
    z_ref[1] = jnp.full(z_ref.shape[1:], MASKED_LOGIT, F32)

    def row_start(t):
        return pl.multiple_of(jnp.minimum(t, n_q - 1) * tile, tile)

    def stage0(tq, tk, p):
        qm = jnp.concatenate([qm_ref[h, pl.ds(row_start(tq), tile), :]
                              for h in range(heads)], axis=0)
        kt = k_ref[pl.ds(row_start(tk), tile), :]
        z = lax.dot_general(qm, kt, (((1,), (1,)), ((), ())), preferred_element_type=F32)
        z_ref[p] = jnp.minimum(z, cm_ref[(tq == tk).astype(jnp.int32)])

    def stage1(tq, tk, p):
        z = z_ref[1 - p]
        sp = jnp.maximum(z, jnp.log(1.0 + jnp.exp(jnp.minimum(z, SOFTPLUS_CLAMP))))
        carry = jnp.where(tq == tk, 0.0, carry_ref[...])
        lbc_ref[p] = z - jnp.concatenate([carry] * (tile // LANES), axis=1)
        hi, lo = _split_bf16(sp)
        hl_ref[p] = jnp.concatenate([hi, lo], axis=1)
        carry = carry + jnp.sum(sp, axis=-1, keepdims=True)
        carry_ref[...] = carry
        return jnp.min(carry) > thr

    def stage2(p):
        suffix = _dot(hl_ref[1 - p], u_ref[...])
        a_ref[1 - p] = jnp.exp(lbc_ref[1 - p] - suffix).astype(BF16)

    def stage3(tq, tk, p):
        vt = v_ref[pl.ds(row_start(tk), tile), :]
        acc = jnp.where(tq == tk, 0.0, acc_ref[...]) + _dot(a_ref[p], vt)
        acc_ref[...] = acc
        r3 = pl.multiple_of(tq * tile, tile)
        for h in range(heads):
            oraw_ref[h, pl.ds(r3, tile), :] = acc[h * tile:(h + 1) * tile]

    def step(state, p):
        q0, k0, q1, k1, q2, k2, q3, k3, idle = state
        rest_is_zero = stage1(q1, k1, p)
        stage2(p)
        stage3(q3, k3, p)
        stage0(q0, k0, p)
        ended = q0 >= n_q
        next_query_tile = (k0 == 0) | (rest_is_zero & (q1 == q0))
        nq = jnp.where(ended | ~next_query_tile, q0, q0 + 1)
        nk = jnp.where(ended, k0, jnp.where(next_query_tile, q0 + 1, k0 - 1))
        return (nq, nk, q0, k0, q1, k1, q2, k2, idle + ended.astype(jnp.int32))

    def body(state):
        for j in range(ATTN_UNROLL):
            state = step(state, j % 2)
        return state

    idle_tile = jnp.int32(n_q)
    zero = jnp.int32(0)
    lax.while_loop(lambda st: st[-1] < 3, body,
                   (zero, zero) + (idle_tile,) * 6 + (zero,))

    def normalise(t, c):
        r = pl.multiple_of(t * NORM_ROWS, NORM_ROWS)
        o = jnp.where(in_head[0], oraw_ref[0, pl.ds(r, NORM_ROWS), :], 0.0)
        for h in range(1, heads):
            o = o + jnp.where(in_head[h], oraw_ref[h, pl.ds(r, NORM_ROWS), :], 0.0)
        ms = _dot_split(o * o, gm_ref[...])
        o_ref[pl.ds(r, NORM_ROWS), :] = (o * lax.rsqrt(ms + RMS_EPS) * g_ref[...]).astype(BF16)
        return c

    lax.fori_loop(0, seq // NORM_ROWS, normalise, 0)


def _underflow_threshold(q, k, head_dim, heads):
    def max_row_norm(t):
        b, s, w = t.shape
        t2 = jnp.square(t.astype(F32)).reshape(b, s, w // head_dim, head_dim)
        return jnp.sqrt(jnp.max(jnp.sum(t2, axis=-1), axis=1))
    bound = max_row_norm(q) * max_row_norm(k)
    bound = jnp.max(bound.reshape(bound.shape[0], -1, heads), axis=-1)
    return bound * LOGIT_BOUND_SLACK + UNDERFLOW_MARGIN


def _attn_call(q, k, v, g_attn):
    bsz, seq, aw = q.shape
    tile = ATTN_TILE
    head_dim = aw // ATTN_HEADS
    heads = LANES // head_dim
    rows = heads * tile
    kern = functools.partial(_attn_kernel, tile=tile, head_dim=head_dim)
    jj = jnp.arange(tile)
    u = (jj[:, None] >= jj[None, :]).astype(BF16)
    u = jnp.concatenate([u, u], axis=0)
    qpos = jnp.arange(rows)[:, None] % tile
    cm = jnp.stack([jnp.full((rows, tile), NO_MASK, F32),
                    jnp.where(jj[None, :] < qpos, NO_MASK, MASKED_LOGIT).astype(F32)])
    seq_spec = pl.BlockSpec((None, seq, LANES), lambda b, p, *_: (b, 0, p))

    def const(shape):
        return pl.BlockSpec(shape, lambda b, p, *_: (0,) * len(shape))

    return pl.pallas_call(
        kern,
        grid_spec=pltpu.PrefetchScalarGridSpec(
            num_scalar_prefetch=1,
            grid=(bsz, aw // LANES),
            in_specs=[seq_spec, seq_spec, seq_spec,
                      const((2 * tile, tile)), const((2, rows, tile)),
                      const((LANES, LANES)),
                      pl.BlockSpec((1, LANES), lambda b, p, *_: (0, p))],
            out_specs=seq_spec,
            scratch_shapes=[pltpu.VMEM((heads, seq, LANES), BF16),
                            pltpu.VMEM((2, rows, tile), F32),
                            pltpu.VMEM((2, rows, 2 * tile), BF16),
                            pltpu.VMEM((2, rows, tile), F32),
                            pltpu.VMEM((2, rows, tile), BF16),
                            pltpu.VMEM((rows, LANES), F32),
                            pltpu.VMEM((rows, LANES), F32),
                            pltpu.VMEM((heads, seq + tile, LANES), F32)]),
        out_shape=jax.ShapeDtypeStruct((bsz, seq, aw), BF16),
        compiler_params=pltpu.CompilerParams(
            dimension_semantics=("arbitrary", "arbitrary"),
            vmem_limit_bytes=VMEM_LIMIT_BYTES),
        name="stickbreak_attn",
    )(_underflow_threshold(q, k, head_dim, heads), q, k, v, u, cm,
      _group_mean_matrix(LANES, heads), g_attn.reshape(1, aw))


def _layer_norm(x, g, b):
    mu = jnp.mean(x, axis=-1, keepdims=True)
    xc = x - mu
    var = jnp.mean(xc * xc, axis=-1, keepdims=True)
    return xc * lax.rsqrt(var + LN_EPS) * g + b


def _mlp_kernel(x_ref, yc_ref, ya_ref, wo_ref, g1_ref, b1_ref, wu_ref, wd_ref,
                g2_ref, b2_ref, o_ref, *, cw, alpha, chunk):
    mix = _dot(yc_ref[...], wo_ref[0:cw, :]) + _dot(ya_ref[...], wo_ref[cw:, :])
    x1 = _layer_norm(alpha * x_ref[...] + mix, g1_ref[...], b1_ref[...])
    x1b = x1.astype(BF16)
    ffn = None
    for c0 in range(0, wu_ref.shape[1], chunk):
        hid = jnp.maximum(_dot(x1b, wu_ref[:, c0:c0 + chunk]), 0.0)
        part = _dot((hid * hid).astype(BF16), wd_ref[c0:c0 + chunk, :])
        ffn = part if ffn is None else ffn + part
    o_ref[...] = _layer_norm(alpha * x1 + ffn, g2_ref[...], b2_ref[...])


def _mlp_call(x2, yc2, ya2, w_out, ln1_g, ln1_b, w_up, w_down, ln2_g, ln2_b, alpha):
    t, d = x2.shape
    cw = yc2.shape[1]
    aw = ya2.shape[1]
    rows = MLP_ROWS
    kern = functools.partial(_mlp_kernel, cw=cw, alpha=alpha, chunk=MLP_CHUNK)

    def const(shape):
        return pl.BlockSpec(shape, lambda i: (0, 0), pipeline_mode=pl.Buffered(1))

    return pl.pallas_call(
        kern,
        grid=(t // rows,),
        in_specs=[
            pl.BlockSpec((rows, d), lambda i: (i, 0)),
            pl.BlockSpec((rows, cw), lambda i: (i, 0)),
            pl.BlockSpec((rows, aw), lambda i: (i, 0)),
            const(w_out.shape), const((1, d)), const((1, d)),
            const(w_up.shape), const(w_down.shape), const((1, d)), const((1, d)),
        ],
        out_specs=pl.BlockSpec((rows, d), lambda i: (i, 0)),
        out_shape=jax.ShapeDtypeStruct((t, d), F32),
        compiler_params=pltpu.CompilerParams(
            dimension_semantics=("arbitrary",),
            vmem_limit_bytes=VMEM_LIMIT_BYTES),
        name="outproj_mlp",
    )(x2, yc2, ya2, w_out.astype(BF16), ln1_g.reshape(1, d), ln1_b.reshape(1, d),
      w_up.astype(BF16), w_down.astype(BF16), ln2_g.reshape(1, d), ln2_b.reshape(1, d))


def kernel(x, w_in, conv_w, g_conv, g_attn, w_out, ln1_g, ln1_b, w_up, w_down, ln2_g, ln2_b):
    bsz, seq, d = x.shape
    depth = w_in.shape[0]
    cw = conv_w.shape[2]
    aw = g_attn.shape[1]
    assert w_in.shape[2] == 3 * cw + 3 * aw and w_out.shape[1] == cw + aw
    assert aw % LANES == 0 and seq % PROJ_ROWS == 0 and seq % NORM_ROWS == 0
    assert seq % ATTN_TILE == 0 and ATTN_TILE % LANES == 0
    alpha = float((2 * depth) ** 0.25)
    for l in range(depth):
        yc, q, k, v = _proj_call(x, w_in[l], conv_w[l], g_conv[l], cw, aw)
        ya = _attn_call(q, k, v, g_attn[l:l + 1])
        x = _mlp_call(x.reshape(bsz * seq, d), yc.reshape(bsz * seq, cw),
                      ya.reshape(bsz * seq, aw), w_out[l], ln1_g[l], ln1_b[l],
                      w_up[l], w_down[l], ln2_g[l], ln2_b[l], alpha
                      ).reshape(bsz, seq, d)
    return x
```

```python
import functools

import jax
import jax.numpy as jnp
from jax import lax
from jax.experimental import pallas as pl
from jax.experimental.pallas import tpu as pltpu

F32 = jnp.float32
BF16 = jnp.bfloat16

CONV_GROUPS = 8
CONV_K = 3
ATTN_HEADS = 8
LN_EPS = 1e-5
RMS_EPS = 1e-6
SOFTPLUS_CLAMP = 40.0
MASKED_LOGIT = -1e4
NO_MASK = 3e38
UNDERFLOW_MARGIN = 110.0
LOGIT_BOUND_SLACK = 1.01

LANES = 128
VMEM_LIMIT_BYTES = 56 * 1024 * 1024

PROJ_ROWS = 512
ATTN_TILE = 256
ATTN_UNROLL = 4
NORM_ROWS = 512
MLP_ROWS = 512
MLP_CHUNK = 1024


def _split_bf16(a):
    hi = a.astype(BF16)
    lo = (a - hi.astype(F32)).astype(BF16)
    return hi, lo


def _dot(a, b):
    return jnp.dot(a, b, preferred_element_type=F32)


def _dot_split(a, b):
    hi, lo = _split_bf16(a)
    return _dot(hi, b) + _dot(lo, b)


def _max_head_norm2(t, head_dim):
    n_heads = t.shape[1] // head_dim
    per_block = LANES // head_dim
    tf = t.astype(F32)
    t2 = tf * tf
    lane = lax.broadcasted_iota(jnp.int32, (1, LANES), 1)
    out_row = lax.broadcasted_iota(jnp.int32, (n_heads, LANES), 0)
    out = jnp.zeros((n_heads, LANES), F32)
    for c in range(t.shape[1] // LANES):
        blk = t2[:, c * LANES:(c + 1) * LANES]
        for j in range(per_block):
            in_head = (lane >= j * head_dim) & (lane < (j + 1) * head_dim)
            norm2 = jnp.sum(jnp.where(in_head, blk, 0.0), axis=-1, keepdims=True)
            out = jnp.where(out_row == c * per_block + j,
                            jnp.max(norm2, axis=0, keepdims=True), out)
    return out


def _proj_kernel(x_ref, w_ref, cw_ref, g_ref, gm_ref,
                 yc_ref, q_ref, k_ref, v_ref, qn_ref, kn_ref, u_ref,
                 *, rows, cw, aw, q_scale, head_dim):
    i = pl.program_id(1)
    xb = x_ref[...].astype(BF16)

    def proj(c0, width):
        return _dot(xb, w_ref[:, c0:c0 + width])

    qb = (proj(3 * cw, aw) * q_scale).astype(BF16)
    kb = proj(3 * cw + aw, aw).astype(BF16)
    q_ref[...] = qb
    k_ref[...] = kb
    v_ref[...] = proj(3 * cw + 2 * aw, aw).astype(BF16)

    @pl.when(i == 0)
    def _():
        qn_ref[...] = jnp.zeros_like(qn_ref)
        kn_ref[...] = jnp.zeros_like(kn_ref)

    qn_ref[...] = jnp.maximum(qn_ref[...], _max_head_norm2(qb, head_dim))
    kn_ref[...] = jnp.maximum(kn_ref[...], _max_head_norm2(kb, head_dim))

    b_gate = proj(0, cw)
    u = proj(cw, cw) * proj(2 * cw, cw)

    @pl.when(i == 0)
    def _():
        u_ref[0:8, :] = jnp.zeros((8, cw), F32)

    @pl.when(i > 0)
    def _():
        u_ref[0:8, :] = u_ref[rows:rows + 8, :]

    u_ref[8:rows + 8, :] = u
    u1 = u_ref[7:rows + 7, :]
    u2 = u_ref[6:rows + 6, :]
    y = b_gate * (cw_ref[0:1, :] * u2 + cw_ref[1:2, :] * u1 + cw_ref[2:3, :] * u)

    ms = _dot_split(y * y, gm_ref[...])
    yc_ref[...] = (y * lax.rsqrt(ms + RMS_EPS) * g_ref[...]).astype(BF16)


def _group_mean_matrix(width, groups):
    gid = jnp.arange(width) // (width // groups)
    return jnp.where(gid[:, None] == gid[None, :], groups / width, 0.0).astype(BF16)


def _proj_call(x, w_in, conv_w, g_conv, cw, aw):
    bsz, seq, d = x.shape
    rows = PROJ_ROWS
    head_dim = aw // ATTN_HEADS
    kern = functools.partial(_proj_kernel, rows=rows, cw=cw, aw=aw,
                             q_scale=head_dim ** -0.5, head_dim=head_dim)
    out_spec = pl.BlockSpec((None, rows, cw), lambda b, i: (b, i, 0))
    out_spec_a = pl.BlockSpec((None, rows, aw), lambda b, i: (b, i, 0))
    norm_spec = pl.BlockSpec((None, ATTN_HEADS, LANES), lambda b, i: (b, 0, 0))
    return pl.pallas_call(
        kern,
        grid=(bsz, seq // rows),
        in_specs=[
            pl.BlockSpec((None, rows, d), lambda b, i: (b, i, 0)),
            pl.BlockSpec(w_in.shape, lambda b, i: (0, 0)),
            pl.BlockSpec(conv_w.shape, lambda b, i: (0, 0)),
            pl.BlockSpec((1, cw), lambda b, i: (0, 0)),
            pl.BlockSpec((cw, cw), lambda b, i: (0, 0)),
        ],
        out_specs=[out_spec, out_spec_a, out_spec_a, out_spec_a, norm_spec, norm_spec],
        out_shape=[jax.ShapeDtypeStruct((bsz, seq, cw), BF16)]
        + [jax.ShapeDtypeStruct((bsz, seq, aw), BF16)] * 3
        + [jax.ShapeDtypeStruct((bsz, ATTN_HEADS, LANES), F32)] * 2,
        scratch_shapes=[pltpu.VMEM((rows + 8, cw), F32)],
        compiler_params=pltpu.CompilerParams(
            dimension_semantics=("arbitrary", "arbitrary"),
            vmem_limit_bytes=VMEM_LIMIT_BYTES),
        name="proj_conv",
    )(x, w_in.astype(BF16), conv_w, g_conv.reshape(1, cw),
      _group_mean_matrix(cw, CONV_GROUPS))


def _attn_kernel(qn_ref, kn_ref, q_ref, k_ref, v_ref, u_ref, cm_ref, gm_ref, g_ref, o_ref,
                 qm_ref, z_ref, hl_ref, lbc_ref, a_ref, carry_ref, acc_ref, oraw_ref,
                 *, tile, head_dim):
    heads = LANES // head_dim
    seq = q_ref.shape[0]
    n_q = seq // tile
    thr = (jnp.max(jnp.sqrt(qn_ref[...] * kn_ref[...])) * LOGIT_BOUND_SLACK
           + UNDERFLOW_MARGIN)
    lane = lax.broadcasted_iota(jnp.int32, (1, LANES), 1)
    in_head = [(lane >= h * head_dim) & (lane < (h + 1) * head_dim)
               for h in range(heads)]
    q = q_ref[...]
    for h in range(heads):
        qm_ref[h] = jnp.where(in_head[h], q, jnp.zeros_like(q))
    carry_ref[...] = jnp.zeros_like(carry_ref)
    acc_ref[...] = jnp.zeros_like(acc_ref)
    a_ref[...] = jnp.zeros_like(a_ref)
    hl_ref[1] = jnp.zeros(hl_ref.shape[1:], BF16)
    lbc_ref[1] = jnp.full(lbc_ref.shape[1:], MASKED_LOGIT, F32)
    z_ref[1] = jnp.full(z_ref.shape[1:], MASKED_LOGIT, F32)

    def row_start(t):
        return pl.multiple_of(jnp.minimum(t, n_q - 1) * tile, tile)

    def stage0(tq, tk, p):
        qm = jnp.concatenate([qm_ref[h, pl.ds(row_start(tq), tile), :]
                              for h in range(heads)], axis=0)
        kt = k_ref[pl.ds(row_start(tk), tile), :]
        z = lax.dot_general(qm, kt, (((1,), (1,)), ((), ())), preferred_element_type=F32)
        z_ref[p] = jnp.minimum(z, cm_ref[(tq == tk).astype(jnp.int32)])

    def stage1(tq, tk, p):
        z = z_ref[1 - p]
        sp = jnp.maximum(z, jnp.log(1.0 + jnp.exp(jnp.minimum(z, SOFTPLUS_CLAMP))))
        carry = jnp.where(tq == tk, 0.0, carry_ref[...])
        lbc_ref[p] = z - jnp.concatenate([carry] * (tile // LANES), axis=1)
        hi, lo = _split_bf16(sp)
        hl_ref[p] = jnp.concatenate([hi, lo], axis=1)
        carry = carry + jnp.sum(sp, axis=-1, keepdims=True)
        carry_ref[...] = carry
        return jnp.min(carry) > thr

    def stage2(p):
        suffix = _dot(hl_ref[1 - p], u_ref[...])
        a_ref[1 - p] = jnp.exp(lbc_ref[1 - p] - suffix).astype(BF16)

    def stage3(tq, tk, p):
        vt = v_ref[pl.ds(row_start(tk), tile), :]
        acc = jnp.where(tq == tk, 0.0, acc_ref[...]) + _dot(a_ref[p], vt)
        acc_ref[...] = acc
        r3 = pl.multiple_of(tq * tile, tile)
        for h in range(heads):
            oraw_ref[h, pl.ds(r3, tile), :] = acc[h * tile:(h + 1) * tile]

    def step(state, p):
        q0, k0, q1, k1, q2, k2, q3, k3, idle = state
        rest_is_zero = stage1(q1, k1, p)
        stage2(p)
        stage3(q3, k3, p)
        stage0(q0, k0, p)
        ended = q0 >= n_q
        next_query_tile = (k0 == 0) | (rest_is_zero & (q1 == q0))
        nq = jnp.where(ended | ~next_query_tile, q0, q0 + 1)
        nk = jnp.where(ended, k0, jnp.where(next_query_tile, q0 + 1, k0 - 1))
        return (nq, nk, q0, k0, q1, k1, q2, k2, idle + ended.astype(jnp.int32))

    def body(state):
        for j in range(ATTN_UNROLL):
            state = step(state, j % 2)
        return state

    idle_tile = jnp.int32(n_q)
    zero = jnp.int32(0)
    lax.while_loop(lambda st: st[-1] < 3, body,
                   (zero, zero) + (idle_tile,) * 6 + (zero,))

    def normalise(t, c):
        r = pl.multiple_of(t * NORM_ROWS, NORM_ROWS)
        o = jnp.where(in_head[0], oraw_ref[0, pl.ds(r, NORM_ROWS), :], 0.0)
        for h in range(1, heads):
            o = o + jnp.where(in_head[h], oraw_ref[h, pl.ds(r, NORM_ROWS), :], 0.0)
        ms = _dot_split(o * o, gm_ref[...])
        o_ref[pl.ds(r, NORM_ROWS), :] = (o * lax.rsqrt(ms + RMS_EPS) * g_ref[...]).astype(BF16)
        return c

    lax.fori_loop(0, seq // NORM_ROWS, normalise, 0)


def _attn_call(q, k, v, qn2, kn2, g_attn):
    bsz, seq, aw = q.shape
    tile = ATTN_TILE
    head_dim = aw // ATTN_HEADS
    heads = LANES // head_dim
    rows = heads * tile
    pairs = aw // LANES
    qn2 = qn2.reshape(bsz, pairs, heads, LANES)
    kn2 = kn2.reshape(bsz, pairs, heads, LANES)
    norm_spec = pl.BlockSpec((None, None, heads, LANES), lambda b, p: (b, p, 0, 0))
    kern = functools.partial(_attn_kernel, tile=tile, head_dim=head_dim)
    jj = jnp.arange(tile)
    u = (jj[:, None] >= jj[None, :]).astype(BF16)
    u = jnp.concatenate([u, u], axis=0)
    qpos = jnp.arange(rows)[:, None] % tile
    cm = jnp.stack([jnp.full((rows, tile), NO_MASK, F32),
                    jnp.where(jj[None, :] < qpos, NO_MASK, MASKED_LOGIT).astype(F32)])
    seq_spec = pl.BlockSpec((None, seq, LANES), lambda b, p: (b, 0, p))

    def const(shape):
        return pl.BlockSpec(shape, lambda b, p: (0,) * len(shape))

    return pl.pallas_call(
        kern,
        grid=(bsz, pairs),
        in_specs=[norm_spec, norm_spec, seq_spec, seq_spec, seq_spec,
                  const((2 * tile, tile)), const((2, rows, tile)),
                  const((LANES, LANES)),
                  pl.BlockSpec((1, LANES), lambda b, p: (0, p))],
        out_specs=seq_spec,
        scratch_shapes=[pltpu.VMEM((heads, seq, LANES), BF16),
                        pltpu.VMEM((2, rows, tile), F32),
                        pltpu.VMEM((2, rows, 2 * tile), BF16),
                        pltpu.VMEM((2, rows, tile), F32),
                        pltpu.VMEM((2, rows, tile), BF16),
                        pltpu.VMEM((rows, LANES), F32),
                        pltpu.VMEM((rows, LANES), F32),
                        pltpu.VMEM((heads, seq + tile, LANES), F32)],
        out_shape=jax.ShapeDtypeStruct((bsz, seq, aw), BF16),
        compiler_params=pltpu.CompilerParams(
            dimension_semantics=("arbitrary", "arbitrary"),
            vmem_limit_bytes=VMEM_LIMIT_BYTES),
        name="stickbreak_attn",
    )(qn2, kn2, q, k, v, u, cm, _group_mean_matrix(LANES, heads), g_attn.reshape(1, aw))


def _layer_norm(x, g, b):
    mu = jnp.mean(x, axis=-1, keepdims=True)
    xc = x - mu
    var = jnp.mean(xc * xc, axis=-1, keepdims=True)
    return xc * lax.rsqrt(var + LN_EPS) * g + b


def _mlp_kernel(x_ref, yc_ref, ya_ref, wo_ref, g1_ref, b1_ref, wu_ref, wd_ref,
                g2_ref, b2_ref, o_ref, *, cw, alpha, chunk):
    mix = _dot(yc_ref[...], wo_ref[0:cw, :]) + _dot(ya_ref[...], wo_ref[cw:, :])
    x1 = _layer_norm(alpha * x_ref[...] + mix, g1_ref[...], b1_ref[...])
    x1b = x1.astype(BF16)
    ffn = None
    for c0 in range(0, wu_ref.shape[1], chunk):
        hid = jnp.maximum(_dot(x1b, wu_ref[:, c0:c0 + chunk]), 0.0)
        part = _dot((hid * hid).astype(BF16), wd_ref[c0:c0 + chunk, :])
        ffn = part if ffn is None else ffn + part
    o_ref[...] = _layer_norm(alpha * x1 + ffn, g2_ref[...], b2_ref[...])


def _mlp_call(x2, yc2, ya2, w_out, ln1_g, ln1_b, w_up, w_down, ln2_g, ln2_b, alpha):
    t, d = x2.shape
    cw = yc2.shape[1]
    aw = ya2.shape[1]
    rows = MLP_ROWS
    kern = functools.partial(_mlp_kernel, cw=cw, alpha=alpha, chunk=MLP_CHUNK)

    def const(shape):
        return pl.BlockSpec(shape, lambda i: (0, 0), pipeline_mode=pl.Buffered(1))

    return pl.pallas_call(
        kern,
        grid=(t // rows,),
        in_specs=[
            pl.BlockSpec((rows, d), lambda i: (i, 0)),
            pl.BlockSpec((rows, cw), lambda i: (i, 0)),
            pl.BlockSpec((rows, aw), lambda i: (i, 0)),
            const(w_out.shape), const((1, d)), const((1, d)),
            const(w_up.shape), const(w_down.shape), const((1, d)), const((1, d)),
        ],
        out_specs=pl.BlockSpec((rows, d), lambda i: (i, 0)),
        out_shape=jax.ShapeDtypeStruct((t, d), F32),
        compiler_params=pltpu.CompilerParams(
            dimension_semantics=("arbitrary",),
            vmem_limit_bytes=VMEM_LIMIT_BYTES),
        name="outproj_mlp",
    )(x2, yc2, ya2, w_out.astype(BF16), ln1_g.reshape(1, d), ln1_b.reshape(1, d),
      w_up.astype(BF16), w_down.astype(BF16), ln2_g.reshape(1, d), ln2_b.reshape(1, d))


def kernel(x, w_in, conv_w, g_conv, g_attn, w_out, ln1_g, ln1_b, w_up, w_down, ln2_g, ln2_b):
    bsz, seq, d = x.shape
    depth = w_in.shape[0]
    cw = conv_w.shape[2]
    aw = g_attn.shape[1]
    assert w_in.shape[2] == 3 * cw + 3 * aw and w_out.shape[1] == cw + aw
    assert aw % LANES == 0 and seq % PROJ_ROWS == 0 and seq % NORM_ROWS == 0
    assert seq % ATTN_TILE == 0 and ATTN_TILE % LANES == 0
    alpha = float((2 * depth) ** 0.25)
    for l in range(depth):
        yc, q, k, v, qn2, kn2 = _proj_call(x, w_in[l], conv_w[l], g_conv[l], cw, aw)
        ya = _attn_call(q, k, v, qn2, kn2, g_attn[l:l + 1])
        x = _mlp_call(x.reshape(bsz * seq, d), yc.reshape(bsz * seq, cw),
                      ya.reshape(bsz * seq, aw), w_out[l], ln1_g[l], ln1_b[l],
                      w_up[l], w_down[l], ln2_g[l], ln2_b[l], alpha
                      ).reshape(bsz, seq, d)
    return x
```

```python
import functools

import jax
import jax.numpy as jnp
from jax import lax
from jax.experimental import pallas as pl
from jax.experimental.pallas import tpu as pltpu

F32 = jnp.float32
BF16 = jnp.bfloat16

CONV_GROUPS = 8
CONV_K = 3
ATTN_HEADS = 8
LN_EPS = 1e-5
RMS_EPS = 1e-6
SOFTPLUS_CLAMP = 40.0
MASKED_LOGIT = -1e4
NO_MASK = 3e38
UNDERFLOW_MARGIN = 110.0
LOGIT_BOUND_SLACK = 1.01

LANES = 128
VMEM_LIMIT_BYTES = 56 * 1024 * 1024

PROJ_ROWS = 512
ATTN_TILE = 256
ATTN_UNROLL = 4
NORM_ROWS = 512
MLP_ROWS = 512
MLP_CHUNK = 1024


def _split_bf16(a):
    hi = a.astype(BF16)
    lo = (a - hi.astype(F32)).astype(BF16)
    return hi, lo


def _dot(a, b):
    return jnp.dot(a, b, preferred_element_type=F32)


def _dot_split(a, b):
    hi, lo = _split_bf16(a)
    return _dot(hi, b) + _dot(lo, b)


def _max_head_norm2(t, head_dim):
    n_heads = t.shape[1] // head_dim
    per_block = LANES // head_dim
    tf = t.astype(F32)
    t2 = tf * tf
    lane = lax.broadcasted_iota(jnp.int32, (1, LANES), 1)
    out_row = lax.broadcasted_iota(jnp.int32, (n_heads, LANES), 0)
    out = jnp.zeros((n_heads, LANES), F32)
    for c in range(t.shape[1] // LANES):
        blk = t2[:, c * LANES:(c + 1) * LANES]
        for j in range(per_block):
            in_head = (lane >= j * head_dim) & (lane < (j + 1) * head_dim)
            norm2 = jnp.sum(jnp.where(in_head, blk, 0.0), axis=-1, keepdims=True)
            out = jnp.where(out_row == c * per_block + j,
                            jnp.max(norm2, axis=0, keepdims=True), out)
    return out


def _proj_kernel(x_ref, w_ref, cw_ref, g_ref, gm_ref,
                 yc_ref, q_ref, k_ref, v_ref, qn_ref, kn_ref, u_ref,
                 *, rows, cw, aw, q_scale, head_dim):
    i = pl.program_id(1)
    xb = x_ref[...].astype(BF16)

    def proj(c0, width):
        return _dot(xb, w_ref[:, c0:c0 + width])

    qb = (proj(3 * cw, aw) * q_scale).astype(BF16)
    kb = proj(3 * cw + aw, aw).astype(BF16)
    q_ref[...] = qb
    k_ref[...] = kb
    v_ref[...] = proj(3 * cw + 2 * aw, aw).astype(BF16)

    @pl.when(i == 0)
    def _():
        qn_ref[...] = jnp.zeros_like(qn_ref)
        kn_ref[...] = jnp.zeros_like(kn_ref)

    qn_ref[...] = jnp.maximum(qn_ref[...], _max_head_norm2(qb, head_dim))
    kn_ref[...] = jnp.maximum(kn_ref[...], _max_head_norm2(kb, head_dim))

    b_gate = proj(0, cw)
    u = proj(cw, cw) * proj(2 * cw, cw)

    @pl.when(i == 0)
    def _():
        u_ref[0:8, :] = jnp.zeros((8, cw), F32)

    @pl.when(i > 0)
    def _():
        u_ref[0:8, :] = u_ref[rows:rows + 8, :]

    u_ref[8:rows + 8, :] = u
    u1 = u_ref[7:rows + 7, :]
    u2 = u_ref[6:rows + 6, :]
    y = b_gate * (cw_ref[0:1, :] * u2 + cw_ref[1:2, :] * u1 + cw_ref[2:3, :] * u)

    ms = _dot_split(y * y, gm_ref[...])
    yc_ref[...] = (y * lax.rsqrt(ms + RMS_EPS) * g_ref[...]).astype(BF16)


def _group_mean_matrix(width, groups):
    gid = jnp.arange(width) // (width // groups)
    return jnp.where(gid[:, None] == gid[None, :], groups / width, 0.0).astype(BF16)


def _proj_call(x, w_in, conv_w, g_conv, cw, aw):
    bsz, seq, d = x.shape
    rows = PROJ_ROWS
    head_dim = aw // ATTN_HEADS
    kern = functools.partial(_proj_kernel, rows=rows, cw=cw, aw=aw,
                             q_scale=head_dim ** -0.5, head_dim=head_dim)
    out_spec = pl.BlockSpec((None, rows, cw), lambda b, i: (b, i, 0))
    out_spec_a = pl.BlockSpec((None, rows, aw), lambda b, i: (b, i, 0))
    norm_spec = pl.BlockSpec((None, ATTN_HEADS, LANES), lambda b, i: (b, 0, 0))
    return pl.pallas_call(
        kern,
        grid=(bsz, seq // rows),
        in_specs=[
            pl.BlockSpec((None, rows, d), lambda b, i: (b, i, 0)),
            pl.BlockSpec(w_in.shape, lambda b, i: (0, 0)),
            pl.BlockSpec(conv_w.shape, lambda b, i: (0, 0)),
            pl.BlockSpec((1, cw), lambda b, i: (0, 0)),
            pl.BlockSpec((cw, cw), lambda b, i: (0, 0)),
        ],
        out_specs=[out_spec, out_spec_a, out_spec_a, out_spec_a, norm_spec, norm_spec],
        out_shape=[jax.ShapeDtypeStruct((bsz, seq, cw), BF16)]
        + [jax.ShapeDtypeStruct((bsz, seq, aw), BF16)] * 3
        + [jax.ShapeDtypeStruct((bsz, ATTN_HEADS, LANES), F32)] * 2,
        scratch_shapes=[pltpu.VMEM((rows + 8, cw), F32)],
        compiler_params=pltpu.CompilerParams(
            dimension_semantics=("arbitrary", "arbitrary"),
            vmem_limit_bytes=VMEM_LIMIT_BYTES),
        name="proj_conv",
    )(x, w_in.astype(BF16), conv_w, g_conv.reshape(1, cw),
      _group_mean_matrix(cw, CONV_GROUPS))


def _attn_kernel(qn_ref, kn_ref, q_ref, k_ref, v_ref, u_ref, cm_ref, gm_ref, g_ref, o_ref,
                 qm_ref, z_ref, hl_ref, lbc_ref, a_ref, carry_ref, acc_ref, oraw_ref,
                 *, tile, head_dim):
    heads = LANES // head_dim
    seq = q_ref.shape[0]
    n_q = seq // tile
    thr = (jnp.max(jnp.sqrt(qn_ref[...] * kn_ref[...])) * LOGIT_BOUND_SLACK
           + UNDERFLOW_MARGIN)
    lane = lax.broadcasted_iota(jnp.int32, (1, LANES), 1)
    in_head = [(lane >= h * head_dim) & (lane < (h + 1) * head_dim)
               for h in range(heads)]
    q = q_ref[...]
    for h in range(heads):
        qm_ref[h] = jnp.where(in_head[h], q, jnp.zeros_like(q))
    carry_ref[...] = jnp.zeros_like(carry_ref)
    acc_ref[...] = jnp.zeros_like(acc_ref)
    a_ref[...] = jnp.zeros_like(a_ref)
    hl_ref[1] = jnp.zeros(hl_ref.shape[1:], BF16)
    lbc_ref[1] = jnp.full(lbc_ref.shape[1:], MASKED_LOGIT, F32)
    z_ref[1] = jnp.full(z_ref.shape[1:], MASKED_LOGIT, F32)

    def row_start(t):
        return pl.multiple_of(jnp.clip(t, 0, n_q - 1) * tile, tile)

    def next_tile(tq, tk, rest_is_zero):
        new_sweep = (tk == 0) | rest_is_zero
        nq = jnp.minimum(jnp.where(new_sweep, tq + 2, tq), n_q)
        nk = jnp.where(nq >= n_q, n_q, jnp.where(new_sweep, nq, tk - 1))
        return nq, nk

    def stage0(tq, tk, p):
        qm = jnp.concatenate([qm_ref[h, pl.ds(row_start(tq), tile), :]
                              for h in range(heads)], axis=0)
        kt = k_ref[pl.ds(row_start(tk), tile), :]
        z = lax.dot_general(qm, kt, (((1,), (1,)), ((), ())), preferred_element_type=F32)
        z_ref[p] = jnp.minimum(z, cm_ref[(tq == tk).astype(jnp.int32)])

    def stage1(tq, tk, p):
        z = z_ref[1 - p]
        sp = jnp.maximum(z, jnp.log(1.0 + jnp.exp(jnp.minimum(z, SOFTPLUS_CLAMP))))
        carry = jnp.where(tq == tk, 0.0, carry_ref[1 - p])
        lbc_ref[p] = z - jnp.concatenate([carry] * (tile // LANES), axis=1)
        hi, lo = _split_bf16(sp)
        hl_ref[p] = jnp.concatenate([hi, lo], axis=1)
        carry = carry + jnp.sum(sp, axis=-1, keepdims=True)
        carry_ref[1 - p] = carry
        return jnp.min(carry) > thr

    def stage2(p):
        suffix = _dot(hl_ref[1 - p], u_ref[...])
        a_ref[1 - p] = jnp.exp(lbc_ref[1 - p] - suffix).astype(BF16)

    def stage3(tq, tk, p):
        vt = v_ref[pl.ds(row_start(tk), tile), :]
        acc = jnp.where(tq == tk, 0.0, acc_ref[1 - p]) + _dot(a_ref[p], vt)
        acc_ref[1 - p] = acc
        idle = (tq < 0) | (tq >= n_q)
        r3 = pl.multiple_of(jnp.where(idle, n_q, tq) * tile, tile)
        for h in range(heads):
            oraw_ref[h, pl.ds(r3, tile), :] = acc[h * tile:(h + 1) * tile]

    def step(state, p):
        q1, k1, q2, k2, q3, k3, rest_is_zero, _ = state
        q0, k0 = next_tile(q2, k2, rest_is_zero)
        other_rest_is_zero = stage1(q1, k1, p)
        stage2(p)
        stage3(q3, k3, p)
        stage0(q0, k0, p)
        flushed = (q0 >= n_q) & (q1 >= n_q) & (q2 >= n_q)
        return (q0, k0, q1, k1, q2, k2, other_rest_is_zero, flushed)

    def body(state):
        for j in range(ATTN_UNROLL):
            state = step(state, j % 2)
        return state

    def not_started(lane):
        return (jnp.int32(lane - 2), jnp.int32(0))

    lax.while_loop(lambda st: jnp.logical_not(st[-1]), body,
                   not_started(1) + not_started(0) + not_started(1)
                   + (jnp.bool_(False), jnp.bool_(False)))

    def normalise(t, c):
        r = pl.multiple_of(t * NORM_ROWS, NORM_ROWS)
        o = jnp.where(in_head[0], oraw_ref[0, pl.ds(r, NORM_ROWS), :], 0.0)
        for h in range(1, heads):
            o = o + jnp.where(in_head[h], oraw_ref[h, pl.ds(r, NORM_ROWS), :], 0.0)
        ms = _dot_split(o * o, gm_ref[...])
        o_ref[pl.ds(r, NORM_ROWS), :] = (o * lax.rsqrt(ms + RMS_EPS) * g_ref[...]).astype(BF16)
        return c

    lax.fori_loop(0, seq // NORM_ROWS, normalise, 0)


def _attn_call(q, k, v, qn2, kn2, g_attn):
    bsz, seq, aw = q.shape
    tile = ATTN_TILE
    head_dim = aw // ATTN_HEADS
    heads = LANES // head_dim
    rows = heads * tile
    pairs = aw // LANES
    qn2 = qn2.reshape(bsz, pairs, heads, LANES)
    kn2 = kn2.reshape(bsz, pairs, heads, LANES)
    norm_spec = pl.BlockSpec((None, None, heads, LANES), lambda b, p: (b, p, 0, 0))
    kern = functools.partial(_attn_kernel, tile=tile, head_dim=head_dim)
    jj = jnp.arange(tile)
    u = (jj[:, None] >= jj[None, :]).astype(BF16)
    u = jnp.concatenate([u, u], axis=0)
    qpos = jnp.arange(rows)[:, None] % tile
    cm = jnp.stack([jnp.full((rows, tile), NO_MASK, F32),
                    jnp.where(jj[None, :] < qpos, NO_MASK, MASKED_LOGIT).astype(F32)])
    seq_spec = pl.BlockSpec((None, seq, LANES), lambda b, p: (b, 0, p))

    def const(shape):
        return pl.BlockSpec(shape, lambda b, p: (0,) * len(shape))

    return pl.pallas_call(
        kern,
        grid=(bsz, pairs),
        in_specs=[norm_spec, norm_spec, seq_spec, seq_spec, seq_spec,
                  const((2 * tile, tile)), const((2, rows, tile)),
                  const((LANES, LANES)),
                  pl.BlockSpec((1, LANES), lambda b, p: (0, p))],
        out_specs=seq_spec,
        scratch_shapes=[pltpu.VMEM((heads, seq, LANES), BF16),
                        pltpu.VMEM((2, rows, tile), F32),
                        pltpu.VMEM((2, rows, 2 * tile), BF16),
                        pltpu.VMEM((2, rows, tile), F32),
                        pltpu.VMEM((2, rows, tile), BF16),
                        pltpu.VMEM((2, rows, LANES), F32),
                        pltpu.VMEM((2, rows, LANES), F32),
                        pltpu.VMEM((heads, seq + tile, LANES), F32)],
        out_shape=jax.ShapeDtypeStruct((bsz, seq, aw), BF16),
        compiler_params=pltpu.CompilerParams(
            dimension_semantics=("arbitrary", "arbitrary"),
            vmem_limit_bytes=VMEM_LIMIT_BYTES),
        name="stickbreak_attn",
    )(qn2, kn2, q, k, v, u, cm, _group_mean_matrix(LANES, heads), g_attn.reshape(1, aw))


def _layer_norm(x, g, b):
    mu = jnp.mean(x, axis=-1, keepdims=True)
    xc = x - mu
    var = jnp.mean(xc * xc, axis=-1, keepdims=True)
    return xc * lax.rsqrt(var + LN_EPS) * g + b


def _mlp_kernel(x_ref, yc_ref, ya_ref, wo_ref, g1_ref, b1_ref, wu_ref, wd_ref,
                g2_ref, b2_ref, o_ref, *, cw, alpha, chunk):
    mix = _dot(yc_ref[...], wo_ref[0:cw, :]) + _dot(ya_ref[...], wo_ref[cw:, :])
    x1 = _layer_norm(alpha * x_ref[...] + mix, g1_ref[...], b1_ref[...])
    x1b = x1.astype(BF16)
    ffn = None
    for c0 in range(0, wu_ref.shape[1], chunk):
        hid = jnp.maximum(_dot(x1b, wu_ref[:, c0:c0 + chunk]), 0.0)
        part = _dot((hid * hid).astype(BF16), wd_ref[c0:c0 + chunk, :])
        ffn = part if ffn is None else ffn + part
    o_ref[...] = _layer_norm(alpha * x1 + ffn, g2_ref[...], b2_ref[...])


def _mlp_call(x2, yc2, ya2, w_out, ln1_g, ln1_b, w_up, w_down, ln2_g, ln2_b, alpha):
    t, d = x2.shape
    cw = yc2.shape[1]
    aw = ya2.shape[1]
    rows = MLP_ROWS
    kern = functools.partial(_mlp_kernel, cw=cw, alpha=alpha, chunk=MLP_CHUNK)

    def const(shape):
        return pl.BlockSpec(shape, lambda i: (0, 0), pipeline_mode=pl.Buffered(1))

    return pl.pallas_call(
        kern,
        grid=(t // rows,),
        in_specs=[
            pl.BlockSpec((rows, d), lambda i: (i, 0)),
            pl.BlockSpec((rows, cw), lambda i: (i, 0)),
            pl.BlockSpec((rows, aw), lambda i: (i, 0)),
            const(w_out.shape), const((1, d)), const((1, d)),
            const(w_up.shape), const(w_down.shape), const((1, d)), const((1, d)),
        ],
        out_specs=pl.BlockSpec((rows, d), lambda i: (i, 0)),
        out_shape=jax.ShapeDtypeStruct((t, d), F32),
        compiler_params=pltpu.CompilerParams(
            dimension_semantics=("arbitrary",),
            vmem_limit_bytes=VMEM_LIMIT_BYTES),
        name="outproj_mlp",
    )(x2, yc2, ya2, w_out.astype(BF16), ln1_g.reshape(1, d), ln1_b.reshape(1, d),
      w_up.astype(BF16), w_down.astype(BF16), ln2_g.reshape(1, d), ln2_b.reshape(1, d))


def kernel(x, w_in, conv_w, g_conv, g_attn, w_out, ln1_g, ln1_b, w_up, w_down, ln2_g, ln2_b):
    bsz, seq, d = x.shape
    depth = w_in.shape[0]
    cw = conv_w.shape[2]
    aw = g_attn.shape[1]
    assert w_in.shape[2] == 3 * cw + 3 * aw and w_out.shape[1] == cw + aw
    assert aw % LANES == 0 and seq % PROJ_ROWS == 0 and seq % NORM_ROWS == 0
    assert seq % ATTN_TILE == 0 and ATTN_TILE % LANES == 0
    alpha = float((2 * depth) ** 0.25)
    for l in range(depth):
        yc, q, k, v, qn2, kn2 = _proj_call(x, w_in[l], conv_w[l], g_conv[l], cw, aw)
        ya = _attn_call(q, k, v, qn2, kn2, g_attn[l:l + 1])
        x = _mlp_call(x.reshape(bsz * seq, d), yc.reshape(bsz * seq, cw),
                      ya.reshape(bsz * seq, aw), w_out[l], ln1_g[l], ln1_b[l],
                      w_up[l], w_down[l], ln2_g[l], ln2_b[l], alpha
                      ).reshape(bsz, seq, d)
    return x
```

```python
import functools

import jax
import jax.numpy as jnp
from jax import lax
from jax.experimental import pallas as pl
from jax.experimental.pallas import tpu as pltpu

F32 = jnp.float32
BF16 = jnp.bfloat16

CONV_GROUPS = 8
CONV_K = 3
ATTN_HEADS = 8
LN_EPS = 1e-5
RMS_EPS = 1e-6
SOFTPLUS_CLAMP = 40.0
MASKED_LOGIT = -1e4
NO_MASK = 3e38
UNDERFLOW_MARGIN = 110.0
LOGIT_BOUND_SLACK = 1.01

LANES = 128
VMEM_LIMIT_BYTES = 56 * 1024 * 1024

PROJ_ROWS = 1024
PROJ_SUB_ROWS = 256
ATTN_TILE = 256
ATTN_UNROLL = 4
NORM_ROWS = 1024
MLP_ROWS = 1024
MLP_SUB_ROWS = 256
MLP_CHUNK = 1024


def _split_bf16(a):
    hi = a.astype(BF16)
    lo = (a - hi.astype(F32)).astype(BF16)
    return hi, lo


def _dot(a, b):
    return jnp.dot(a, b, preferred_element_type=F32)


def _dot_split(a, b):
    hi, lo = _split_bf16(a)
    return _dot(hi, b) + _dot(lo, b)


def _max_head_norm2(t, head_dim):
    n_heads = t.shape[1] // head_dim
    per_block = LANES // head_dim
    tf = t.astype(F32)
    t2 = tf * tf
    lane = lax.broadcasted_iota(jnp.int32, (1, LANES), 1)
    out_row = lax.broadcasted_iota(jnp.int32, (n_heads, LANES), 0)
    out = jnp.zeros((n_heads, LANES), F32)
    for c in range(t.shape[1] // LANES):
        blk = t2[:, c * LANES:(c + 1) * LANES]
        for j in range(per_block):
            in_head = (lane >= j * head_dim) & (lane < (j + 1) * head_dim)
            norm2 = jnp.sum(jnp.where(in_head, blk, 0.0), axis=-1, keepdims=True)
            out = jnp.where(out_row == c * per_block + j,
                            jnp.max(norm2, axis=0, keepdims=True), out)
    return out


def _proj_kernel(x_ref, w_ref, cw_ref, g_ref, gm_ref,
                 yc_ref, q_ref, k_ref, v_ref, qn_ref, kn_ref, u_ref,
                 *, rows, cw, aw, q_scale, head_dim):
    i = pl.program_id(1)
    sub = PROJ_SUB_ROWS

    @pl.when(i == 0)
    def _():
        u_ref[0:8, :] = jnp.zeros((8, cw), F32)
        qn_ref[...] = jnp.zeros_like(qn_ref)
        kn_ref[...] = jnp.zeros_like(kn_ref)

    @pl.when(i > 0)
    def _():
        u_ref[0:8, :] = u_ref[rows:rows + 8, :]

    qn = qn_ref[...]
    kn = kn_ref[...]
    gates = []
    for r0 in range(0, rows, sub):
        xb = x_ref[r0:r0 + sub, :].astype(BF16)

        def proj(c0, width, xb=xb):
            return _dot(xb, w_ref[:, c0:c0 + width])

        qb = (proj(3 * cw, aw) * q_scale).astype(BF16)
        kb = proj(3 * cw + aw, aw).astype(BF16)
        q_ref[r0:r0 + sub, :] = qb
        k_ref[r0:r0 + sub, :] = kb
        v_ref[r0:r0 + sub, :] = proj(3 * cw + 2 * aw, aw).astype(BF16)
        qn = jnp.maximum(qn, _max_head_norm2(qb, head_dim))
        kn = jnp.maximum(kn, _max_head_norm2(kb, head_dim))
        b_gate = proj(0, cw)
        u = proj(cw, cw) * proj(2 * cw, cw)
        u_ref[8 + r0:8 + r0 + sub, :] = u
        gates.append((b_gate, u))
    qn_ref[...] = qn
    kn_ref[...] = kn

    for r0, (b_gate, u) in zip(range(0, rows, sub), gates):
        u1 = u_ref[7 + r0:7 + r0 + sub, :]
        u2 = u_ref[6 + r0:6 + r0 + sub, :]
        y = b_gate * (cw_ref[0:1, :] * u2 + cw_ref[1:2, :] * u1 + cw_ref[2:3, :] * u)
        ms = _dot_split(y * y, gm_ref[...])
        yc_ref[r0:r0 + sub, :] = (y * lax.rsqrt(ms + RMS_EPS) * g_ref[...]).astype(BF16)


def _group_mean_matrix(width, groups):
    gid = jnp.arange(width) // (width // groups)
    return jnp.where(gid[:, None] == gid[None, :], groups / width, 0.0).astype(BF16)


def _proj_call(x, w_in, conv_w, g_conv, cw, aw):
    bsz, seq, d = x.shape
    rows = PROJ_ROWS
    head_dim = aw // ATTN_HEADS
    kern = functools.partial(_proj_kernel, rows=rows, cw=cw, aw=aw,
                             q_scale=head_dim ** -0.5, head_dim=head_dim)
    out_spec = pl.BlockSpec((None, rows, cw), lambda b, i: (b, i, 0))
    out_spec_a = pl.BlockSpec((None, rows, aw), lambda b, i: (b, i, 0))
    norm_spec = pl.BlockSpec((None, ATTN_HEADS, LANES), lambda b, i: (b, 0, 0))
    return pl.pallas_call(
        kern,
        grid=(bsz, seq // rows),
        in_specs=[
            pl.BlockSpec((None, rows, d), lambda b, i: (b, i, 0)),
            pl.BlockSpec(w_in.shape, lambda b, i: (0, 0)),
            pl.BlockSpec(conv_w.shape, lambda b, i: (0, 0)),
            pl.BlockSpec((1, cw), lambda b, i: (0, 0)),
            pl.BlockSpec((cw, cw), lambda b, i: (0, 0)),
        ],
        out_specs=[out_spec, out_spec_a, out_spec_a, out_spec_a, norm_spec, norm_spec],
        out_shape=[jax.ShapeDtypeStruct((bsz, seq, cw), BF16)]
        + [jax.ShapeDtypeStruct((bsz, seq, aw), BF16)] * 3
        + [jax.ShapeDtypeStruct((bsz, ATTN_HEADS, LANES), F32)] * 2,
        scratch_shapes=[pltpu.VMEM((rows + 8, cw), F32)],
        compiler_params=pltpu.CompilerParams(
            dimension_semantics=("arbitrary", "arbitrary"),
            vmem_limit_bytes=VMEM_LIMIT_BYTES),
        name="proj_conv",
    )(x, w_in.astype(BF16), conv_w, g_conv.reshape(1, cw),
      _group_mean_matrix(cw, CONV_GROUPS))


def _attn_kernel(qn_ref, kn_ref, q_ref, k_ref, v_ref, u_ref, cm_ref, gm_ref, g_ref, o_ref,
                 qm_ref, z_ref, hl_ref, lbc_ref, a_ref, carry_ref, acc_ref, oraw_ref,
                 *, tile, head_dim):
    heads = LANES // head_dim
    seq = q_ref.shape[0]
    n_q = seq // tile
    thr = (jnp.max(jnp.sqrt(qn_ref[...] * kn_ref[...])) * LOGIT_BOUND_SLACK
           + UNDERFLOW_MARGIN)
    lane = lax.broadcasted_iota(jnp.int32, (1, LANES), 1)
    in_head = [(lane >= h * head_dim) & (lane < (h + 1) * head_dim)
               for h in range(heads)]
    def mask_heads(t, c):
        r = pl.multiple_of(t * tile, tile)
        q = q_ref[pl.ds(r, tile), :]
        for h in range(heads):
            qm_ref[h, pl.ds(r, tile), :] = jnp.where(in_head[h], q, jnp.zeros_like(q))
        return c

    lax.fori_loop(0, n_q, mask_heads, 0)
    carry_ref[...] = jnp.zeros_like(carry_ref)
    acc_ref[...] = jnp.zeros_like(acc_ref)
    a_ref[...] = jnp.zeros_like(a_ref)
    hl_ref[1] = jnp.zeros(hl_ref.shape[1:], BF16)
    lbc_ref[1] = jnp.full(lbc_ref.shape[1:], MASKED_LOGIT, F32)
    z_ref[1] = jnp.full(z_ref.shape[1:], MASKED_LOGIT, F32)

    def row_start(t):
        return pl.multiple_of(jnp.clip(t, 0, n_q - 1) * tile, tile)

    def next_tile(tq, tk, rest_is_zero):
        new_sweep = (tk == 0) | rest_is_zero
        nq = jnp.minimum(jnp.where(new_sweep, tq + 2, tq), n_q)
        nk = jnp.where(nq >= n_q, n_q, jnp.where(new_sweep, nq, tk - 1))
        return nq, nk

    def stage0(tq, tk, p):
        qm = jnp.concatenate([qm_ref[h, pl.ds(row_start(tq), tile), :]
                              for h in range(heads)], axis=0)
        kt = k_ref[pl.ds(row_start(tk), tile), :]
        z = lax.dot_general(qm, kt, (((1,), (1,)), ((), ())), preferred_element_type=F32)
        z_ref[p] = jnp.minimum(z, cm_ref[(tq == tk).astype(jnp.int32)])

    def stage1(tq, tk, p):
        z = z_ref[1 - p]
        sp = jnp.maximum(z, jnp.log(1.0 + jnp.exp(jnp.minimum(z, SOFTPLUS_CLAMP))))
        carry = jnp.where(tq == tk, 0.0, carry_ref[1 - p])
        lbc_ref[p] = z - jnp.concatenate([carry] * (tile // LANES), axis=1)
        hi, lo = _split_bf16(sp)
        hl_ref[p] = jnp.concatenate([hi, lo], axis=1)
        carry = carry + jnp.sum(sp, axis=-1, keepdims=True)
        carry_ref[1 - p] = carry
        return jnp.min(carry) > thr

    def stage2(p):
        suffix = _dot(hl_ref[1 - p], u_ref[...])
        a_ref[1 - p] = jnp.exp(lbc_ref[1 - p] - suffix).astype(BF16)

    def stage3(tq, tk, p):
        vt = v_ref[pl.ds(row_start(tk), tile), :]
        acc = jnp.where(tq == tk, 0.0, acc_ref[1 - p]) + _dot(a_ref[p], vt)
        acc_ref[1 - p] = acc
        idle = (tq < 0) | (tq >= n_q)
        r3 = pl.multiple_of(jnp.where(idle, n_q, tq) * tile, tile)
        for h in range(heads):
            oraw_ref[h, pl.ds(r3, tile), :] = acc[h * tile:(h + 1) * tile]

    def step(state, p):
        q1, k1, q2, k2, q3, k3, rest_is_zero, _ = state
        q0, k0 = next_tile(q2, k2, rest_is_zero)
        other_rest_is_zero = stage1(q1, k1, p)
        stage2(p)
        stage3(q3, k3, p)
        stage0(q0, k0, p)
        flushed = (q0 >= n_q) & (q1 >= n_q) & (q2 >= n_q)
        return (q0, k0, q1, k1, q2, k2, other_rest_is_zero, flushed)

    def body(state):
        for j in range(ATTN_UNROLL):
            state = step(state, j % 2)
        return state

    def not_started(lane):
        return (jnp.int32(lane - 2), jnp.int32(0))

    lax.while_loop(lambda st: jnp.logical_not(st[-1]), body,
                   not_started(1) + not_started(0) + not_started(1)
                   + (jnp.bool_(False), jnp.bool_(False)))

    def normalise(t, c):
        r = pl.multiple_of(t * NORM_ROWS, NORM_ROWS)
        o = jnp.where(in_head[0], oraw_ref[0, pl.ds(r, NORM_ROWS), :], 0.0)
        for h in range(1, heads):
            o = o + jnp.where(in_head[h], oraw_ref[h, pl.ds(r, NORM_ROWS), :], 0.0)
        ms = _dot_split(o * o, gm_ref[...])
        o_ref[pl.ds(r, NORM_ROWS), :] = (o * lax.rsqrt(ms + RMS_EPS) * g_ref[...]).astype(BF16)
        return c

    lax.fori_loop(0, seq // NORM_ROWS, normalise, 0)


def _attn_call(q, k, v, qn2, kn2, g_attn):
    bsz, seq, aw = q.shape
    tile = ATTN_TILE
    head_dim = aw // ATTN_HEADS
    heads = LANES // head_dim
    rows = heads * tile
    pairs = aw // LANES
    qn2 = qn2.reshape(bsz, pairs, heads, LANES)
    kn2 = kn2.reshape(bsz, pairs, heads, LANES)
    norm_spec = pl.BlockSpec((None, None, heads, LANES), lambda b, p: (b, p, 0, 0))
    kern = functools.partial(_attn_kernel, tile=tile, head_dim=head_dim)
    jj = jnp.arange(tile)
    u = (jj[:, None] >= jj[None, :]).astype(BF16)
    u = jnp.concatenate([u, u], axis=0)
    qpos = jnp.arange(rows)[:, None] % tile
    cm = jnp.stack([jnp.full((rows, tile), NO_MASK, F32),
                    jnp.where(jj[None, :] < qpos, NO_MASK, MASKED_LOGIT).astype(F32)])
    seq_spec = pl.BlockSpec((None, seq, LANES), lambda b, p: (b, 0, p))

    def const(shape):
        return pl.BlockSpec(shape, lambda b, p: (0,) * len(shape))

    return pl.pallas_call(
        kern,
        grid=(bsz, pairs),
        in_specs=[norm_spec, norm_spec, seq_spec, seq_spec, seq_spec,
                  const((2 * tile, tile)), const((2, rows, tile)),
                  const((LANES, LANES)),
                  pl.BlockSpec((1, LANES), lambda b, p: (0, p))],
        out_specs=seq_spec,
        scratch_shapes=[pltpu.VMEM((heads, seq, LANES), BF16),
                        pltpu.VMEM((2, rows, tile), F32),
                        pltpu.VMEM((2, rows, 2 * tile), BF16),
                        pltpu.VMEM((2, rows, tile), F32),
                        pltpu.VMEM((2, rows, tile), BF16),
                        pltpu.VMEM((2, rows, LANES), F32),
                        pltpu.VMEM((2, rows, LANES), F32),
                        pltpu.VMEM((heads, seq + tile, LANES), F32)],
        out_shape=jax.ShapeDtypeStruct((bsz, seq, aw), BF16),
        compiler_params=pltpu.CompilerParams(
            dimension_semantics=("arbitrary", "arbitrary"),
            vmem_limit_bytes=VMEM_LIMIT_BYTES),
        name="stickbreak_attn",
    )(qn2, kn2, q, k, v, u, cm, _group_mean_matrix(LANES, heads), g_attn.reshape(1, aw))


def _layer_norm(x, g, b):
    mu = jnp.mean(x, axis=-1, keepdims=True)
    xc = x - mu
    var = jnp.mean(xc * xc, axis=-1, keepdims=True)
    return xc * lax.rsqrt(var + LN_EPS) * g + b


def _mlp_kernel(x_ref, yc_ref, ya_ref, wo_ref, g1_ref, b1_ref, wu_ref, wd_ref,
                g2_ref, b2_ref, o_ref, *, cw, alpha, chunk):
    blocks = [slice(r, r + MLP_SUB_ROWS) for r in range(0, x_ref.shape[0], MLP_SUB_ROWS)]
    mix = [_dot(yc_ref[b, :], wo_ref[0:cw, :]) + _dot(ya_ref[b, :], wo_ref[cw:, :])
           for b in blocks]
    for b, mix_b in zip(blocks, mix):
        x1 = _layer_norm(alpha * x_ref[b, :] + mix_b, g1_ref[...], b1_ref[...])
        x1b = x1.astype(BF16)
        ffn = None
        for c0 in range(0, wu_ref.shape[1], chunk):
            hid = jnp.maximum(_dot(x1b, wu_ref[:, c0:c0 + chunk]), 0.0)
            part = _dot((hid * hid).astype(BF16), wd_ref[c0:c0 + chunk, :])
            ffn = part if ffn is None else ffn + part
        o_ref[b, :] = _layer_norm(alpha * x1 + ffn, g2_ref[...], b2_ref[...])


def _mlp_call(x2, yc2, ya2, w_out, ln1_g, ln1_b, w_up, w_down, ln2_g, ln2_b, alpha):
    t, d = x2.shape
    cw = yc2.shape[1]
    aw = ya2.shape[1]
    rows = MLP_ROWS
    kern = functools.partial(_mlp_kernel, cw=cw, alpha=alpha, chunk=MLP_CHUNK)

    def const(shape):
        return pl.BlockSpec(shape, lambda i: (0, 0), pipeline_mode=pl.Buffered(1))

    return pl.pallas_call(
        kern,
        grid=(t // rows,),
        in_specs=[
            pl.BlockSpec((rows, d), lambda i: (i, 0)),
            pl.BlockSpec((rows, cw), lambda i: (i, 0)),
            pl.BlockSpec((rows, aw), lambda i: (i, 0)),
            const(w_out.shape), const((1, d)), const((1, d)),
            const(w_up.shape), const(w_down.shape), const((1, d)), const((1, d)),
        ],
        out_specs=pl.BlockSpec((rows, d), lambda i: (i, 0)),
        out_shape=jax.ShapeDtypeStruct((t, d), F32),
        compiler_params=pltpu.CompilerParams(
            dimension_semantics=("arbitrary",),
            vmem_limit_bytes=VMEM_LIMIT_BYTES),
        name="outproj_mlp",
    )(x2, yc2, ya2, w_out.astype(BF16), ln1_g.reshape(1, d), ln1_b.reshape(1, d),
      w_up.astype(BF16), w_down.astype(BF16), ln2_g.reshape(1, d), ln2_b.reshape(1, d))


def kernel(x, w_in, conv_w, g_conv, g_attn, w_out, ln1_g, ln1_b, w_up, w_down, ln2_g, ln2_b):
    bsz, seq, d = x.shape
    depth = w_in.shape[0]
    cw = conv_w.shape[2]
    aw = g_attn.shape[1]
    assert w_in.shape[2] == 3 * cw + 3 * aw and w_out.shape[1] == cw + aw
    assert aw % LANES == 0 and seq % PROJ_ROWS == 0 and seq % NORM_ROWS == 0
    assert seq % ATTN_TILE == 0 and ATTN_TILE % LANES == 0
    alpha = float((2 * depth) ** 0.25)
    for l in range(depth):
        yc, q, k, v, qn2, kn2 = _proj_call(x, w_in[l], conv_w[l], g_conv[l], cw, aw)
        ya = _attn_call(q, k, v, qn2, kn2, g_attn[l:l + 1])
        x = _mlp_call(x.reshape(bsz * seq, d), yc.reshape(bsz * seq, cw),
                      ya.reshape(bsz * seq, aw), w_out[l], ln1_g[l], ln1_b[l],
                      w_up[l], w_down[l], ln2_g[l], ln2_b[l], alpha
                      ).reshape(bsz, seq, d)
    return x
```

```python
import functools

import jax
import jax.numpy as jnp
from jax import lax
from jax.experimental import pallas as pl
from jax.experimental.pallas import tpu as pltpu

F32 = jnp.float32
BF16 = jnp.bfloat16

CONV_GROUPS = 8
CONV_K = 3
ATTN_HEADS = 8
LN_EPS = 1e-5
RMS_EPS = 1e-6
SOFTPLUS_CLAMP = 40.0
MASKED_LOGIT = -1e4
NO_MASK = 3e38
UNDERFLOW_MARGIN = 110.0
LOGIT_BOUND_SLACK = 1.01

LANES = 128
VMEM_LIMIT_BYTES = 56 * 1024 * 1024

PROJ_ROWS = 1024
PROJ_SUB_ROWS = 256
ATTN_TILE = 256
ATTN_UNROLL = 6
NORM_ROWS = 1024
MLP_ROWS = 1024
MLP_SUB_ROWS = 256
MLP_CHUNK = 1024


def _split_bf16(a):
    hi = a.astype(BF16)
    lo = (a - hi.astype(F32)).astype(BF16)
    return hi, lo


def _dot(a, b):
    return jnp.dot(a, b, preferred_element_type=F32)


def _dot_split(a, b):
    hi, lo = _split_bf16(a)
    return _dot(hi, b) + _dot(lo, b)


def _max_head_norm2(t, head_dim):
    n_heads = t.shape[1] // head_dim
    per_block = LANES // head_dim
    tf = t.astype(F32)
    t2 = tf * tf
    lane = lax.broadcasted_iota(jnp.int32, (1, LANES), 1)
    out_row = lax.broadcasted_iota(jnp.int32, (n_heads, LANES), 0)
    out = jnp.zeros((n_heads, LANES), F32)
    for c in range(t.shape[1] // LANES):
        blk = t2[:, c * LANES:(c + 1) * LANES]
        for j in range(per_block):
            in_head = (lane >= j * head_dim) & (lane < (j + 1) * head_dim)
            norm2 = jnp.sum(jnp.where(in_head, blk, 0.0), axis=-1, keepdims=True)
            out = jnp.where(out_row == c * per_block + j,
                            jnp.max(norm2, axis=0, keepdims=True), out)
    return out


def _proj_kernel(x_ref, w_ref, cw_ref, g_ref, gm_ref,
                 yc_ref, q_ref, k_ref, v_ref, qn_ref, kn_ref, u_ref,
                 *, rows, cw, aw, q_scale, head_dim):
    i = pl.program_id(1)
    sub = PROJ_SUB_ROWS

    @pl.when(i == 0)
    def _():
        u_ref[0:8, :] = jnp.zeros((8, cw), F32)
        qn_ref[...] = jnp.zeros_like(qn_ref)
        kn_ref[...] = jnp.zeros_like(kn_ref)

    @pl.when(i > 0)
    def _():
        u_ref[0:8, :] = u_ref[rows:rows + 8, :]

    qn = qn_ref[...]
    kn = kn_ref[...]
    gates = []
    for r0 in range(0, rows, sub):
        xb = x_ref[r0:r0 + sub, :].astype(BF16)

        def proj(c0, width, xb=xb):
            return _dot(xb, w_ref[:, c0:c0 + width])

        qb = (proj(3 * cw, aw) * q_scale).astype(BF16)
        kb = proj(3 * cw + aw, aw).astype(BF16)
        q_ref[r0:r0 + sub, :] = qb
        k_ref[r0:r0 + sub, :] = kb
        v_ref[r0:r0 + sub, :] = proj(3 * cw + 2 * aw, aw).astype(BF16)
        qn = jnp.maximum(qn, _max_head_norm2(qb, head_dim))
        kn = jnp.maximum(kn, _max_head_norm2(kb, head_dim))
        b_gate = proj(0, cw)
        u = proj(cw, cw) * proj(2 * cw, cw)
        u_ref[8 + r0:8 + r0 + sub, :] = u
        gates.append((b_gate, u))
    qn_ref[...] = qn
    kn_ref[...] = kn

    for r0, (b_gate, u) in zip(range(0, rows, sub), gates):
        u1 = u_ref[7 + r0:7 + r0 + sub, :]
        u2 = u_ref[6 + r0:6 + r0 + sub, :]
        y = b_gate * (cw_ref[0:1, :] * u2 + cw_ref[1:2, :] * u1 + cw_ref[2:3, :] * u)
        ms = _dot_split(y * y, gm_ref[...])
        yc_ref[r0:r0 + sub, :] = (y * lax.rsqrt(ms + RMS_EPS) * g_ref[...]).astype(BF16)


def _group_mean_matrix(width, groups):
    gid = jnp.arange(width) // (width // groups)
    return jnp.where(gid[:, None] == gid[None, :], groups / width, 0.0).astype(BF16)


def _proj_call(x, w_in, conv_w, g_conv, cw, aw):
    bsz, seq, d = x.shape
    rows = PROJ_ROWS
    head_dim = aw // ATTN_HEADS
    kern = functools.partial(_proj_kernel, rows=rows, cw=cw, aw=aw,
                             q_scale=head_dim ** -0.5, head_dim=head_dim)
    out_spec = pl.BlockSpec((None, rows, cw), lambda b, i: (b, i, 0))
    out_spec_a = pl.BlockSpec((None, rows, aw), lambda b, i: (b, i, 0))
    norm_spec = pl.BlockSpec((None, ATTN_HEADS, LANES), lambda b, i: (b, 0, 0))
    return pl.pallas_call(
        kern,
        grid=(bsz, seq // rows),
        in_specs=[
            pl.BlockSpec((None, rows, d), lambda b, i: (b, i, 0)),
            pl.BlockSpec(w_in.shape, lambda b, i: (0, 0)),
            pl.BlockSpec(conv_w.shape, lambda b, i: (0, 0)),
            pl.BlockSpec((1, cw), lambda b, i: (0, 0)),
            pl.BlockSpec((cw, cw), lambda b, i: (0, 0)),
        ],
        out_specs=[out_spec, out_spec_a, out_spec_a, out_spec_a, norm_spec, norm_spec],
        out_shape=[jax.ShapeDtypeStruct((bsz, seq, cw), BF16)]
        + [jax.ShapeDtypeStruct((bsz, seq, aw), BF16)] * 3
        + [jax.ShapeDtypeStruct((bsz, ATTN_HEADS, LANES), F32)] * 2,
        scratch_shapes=[pltpu.VMEM((rows + 8, cw), F32)],
        compiler_params=pltpu.CompilerParams(
            dimension_semantics=("arbitrary", "arbitrary"),
            vmem_limit_bytes=VMEM_LIMIT_BYTES),
        name="proj_conv",
    )(x, w_in.astype(BF16), conv_w, g_conv.reshape(1, cw),
      _group_mean_matrix(cw, CONV_GROUPS))


def _attn_kernel(qn_ref, kn_ref, q_ref, k_ref, v_ref, u_ref, cm_ref, gm_ref, g_ref, o_ref,
                 qm_ref, z_ref, hl_ref, lbc_ref, a_ref, carry_ref, acc_ref, oraw_ref,
                 *, tile, head_dim):
    heads = LANES // head_dim
    seq = q_ref.shape[0]
    n_q = seq // tile
    thr = (jnp.max(jnp.sqrt(qn_ref[...] * kn_ref[...])) * LOGIT_BOUND_SLACK
           + UNDERFLOW_MARGIN)
    lane = lax.broadcasted_iota(jnp.int32, (1, LANES), 1)
    in_head = [(lane >= h * head_dim) & (lane < (h + 1) * head_dim)
               for h in range(heads)]
    def mask_heads(t, c):
        r = pl.multiple_of(t * tile, tile)
        q = q_ref[pl.ds(r, tile), :]
        for h in range(heads):
            qm_ref[h, pl.ds(r, tile), :] = jnp.where(in_head[h], q, jnp.zeros_like(q))
        return c

    lax.fori_loop(0, n_q, mask_heads, 0)
    carry_ref[...] = jnp.zeros_like(carry_ref)
    acc_ref[...] = jnp.zeros_like(acc_ref)
    a_ref[...] = jnp.zeros_like(a_ref)
    hl_ref[1] = jnp.zeros(hl_ref.shape[1:], BF16)
    lbc_ref[1] = jnp.full(lbc_ref.shape[1:], MASKED_LOGIT, F32)
    z_ref[1] = jnp.full(z_ref.shape[1:], MASKED_LOGIT, F32)

    def row_start(t):
        return pl.multiple_of(jnp.clip(t, 0, n_q - 1) * tile, tile)

    def next_tile(tq, tk, rest_is_zero):
        new_sweep = (tk == 0) | rest_is_zero
        nq = jnp.minimum(jnp.where(new_sweep, tq + 2, tq), n_q)
        nk = jnp.where(nq >= n_q, n_q, jnp.where(new_sweep, nq, tk - 1))
        return nq, nk

    def stage0(tq, tk, p):
        qm = jnp.concatenate([qm_ref[h, pl.ds(row_start(tq), tile), :]
                              for h in range(heads)], axis=0)
        kt = k_ref[pl.ds(row_start(tk), tile), :]
        z = lax.dot_general(qm, kt, (((1,), (1,)), ((), ())), preferred_element_type=F32)
        z_ref[p] = jnp.minimum(z, cm_ref[(tq == tk).astype(jnp.int32)])

    def stage1(tq, tk, p):
        z = z_ref[1 - p]
        sp = jnp.maximum(z, jnp.log(1.0 + jnp.exp(jnp.minimum(z, SOFTPLUS_CLAMP))))
        carry = jnp.where(tq == tk, 0.0, carry_ref[1 - p])
        lbc_ref[p] = z - jnp.concatenate([carry] * (tile // LANES), axis=1)
        hl_ref[p] = sp.astype(BF16)
        carry = carry + jnp.sum(sp, axis=-1, keepdims=True)
        carry_ref[1 - p] = carry
        return jnp.min(carry) > thr

    def stage2(p):
        suffix = _dot(hl_ref[1 - p], u_ref[...])
        a_ref[1 - p] = jnp.exp(lbc_ref[1 - p] - suffix).astype(BF16)

    def stage3(tq, tk, p):
        vt = v_ref[pl.ds(row_start(tk), tile), :]
        acc = jnp.where(tq == tk, 0.0, acc_ref[1 - p]) + _dot(a_ref[p], vt)
        acc_ref[1 - p] = acc
        idle = (tq < 0) | (tq >= n_q)
        r3 = pl.multiple_of(jnp.where(idle, n_q, tq) * tile, tile)
        for h in range(heads):
            oraw_ref[h, pl.ds(r3, tile), :] = acc[h * tile:(h + 1) * tile]

    def step(state, p):
        q1, k1, q2, k2, q3, k3, rest_is_zero, _ = state
        q0, k0 = next_tile(q2, k2, rest_is_zero)
        other_rest_is_zero = stage1(q1, k1, p)
        stage2(p)
        stage3(q3, k3, p)
        stage0(q0, k0, p)
        flushed = (q0 >= n_q) & (q1 >= n_q) & (q2 >= n_q)
        return (q0, k0, q1, k1, q2, k2, other_rest_is_zero, flushed)

    def body(state):
        for j in range(ATTN_UNROLL):
            state = step(state, j % 2)
        return state

    def not_started(lane):
        return (jnp.int32(lane - 2), jnp.int32(0))

    lax.while_loop(lambda st: jnp.logical_not(st[-1]), body,
                   not_started(1) + not_started(0) + not_started(1)
                   + (jnp.bool_(False), jnp.bool_(False)))

    def normalise(t, c):
        r = pl.multiple_of(t * NORM_ROWS, NORM_ROWS)
        o = jnp.where(in_head[0], oraw_ref[0, pl.ds(r, NORM_ROWS), :], 0.0)
        for h in range(1, heads):
            o = o + jnp.where(in_head[h], oraw_ref[h, pl.ds(r, NORM_ROWS), :], 0.0)
        ms = _dot_split(o * o, gm_ref[...])
        o_ref[pl.ds(r, NORM_ROWS), :] = (o * lax.rsqrt(ms + RMS_EPS) * g_ref[...]).astype(BF16)
        return c

    lax.fori_loop(0, seq // NORM_ROWS, normalise, 0)


def _attn_call(q, k, v, qn2, kn2, g_attn):
    bsz, seq, aw = q.shape
    tile = ATTN_TILE
    head_dim = aw // ATTN_HEADS
    heads = LANES // head_dim
    rows = heads * tile
    pairs = aw // LANES
    qn2 = qn2.reshape(bsz, pairs, heads, LANES)
    kn2 = kn2.reshape(bsz, pairs, heads, LANES)
    norm_spec = pl.BlockSpec((None, None, heads, LANES), lambda b, p: (b, p, 0, 0))
    kern = functools.partial(_attn_kernel, tile=tile, head_dim=head_dim)
    jj = jnp.arange(tile)
    u = (jj[:, None] >= jj[None, :]).astype(BF16)
    qpos = jnp.arange(rows)[:, None] % tile
    cm = jnp.stack([jnp.full((rows, tile), NO_MASK, F32),
                    jnp.where(jj[None, :] < qpos, NO_MASK, MASKED_LOGIT).astype(F32)])
    seq_spec = pl.BlockSpec((None, seq, LANES), lambda b, p: (b, 0, p))

    def const(shape):
        return pl.BlockSpec(shape, lambda b, p: (0,) * len(shape))

    return pl.pallas_call(
        kern,
        grid=(bsz, pairs),
        in_specs=[norm_spec, norm_spec, seq_spec, seq_spec, seq_spec,
                  const((tile, tile)), const((2, rows, tile)),
                  const((LANES, LANES)),
                  pl.BlockSpec((1, LANES), lambda b, p: (0, p))],
        out_specs=seq_spec,
        scratch_shapes=[pltpu.VMEM((heads, seq, LANES), BF16),
                        pltpu.VMEM((2, rows, tile), F32),
                        pltpu.VMEM((2, rows, tile), BF16),
                        pltpu.VMEM((2, rows, tile), F32),
                        pltpu.VMEM((2, rows, tile), BF16),
                        pltpu.VMEM((2, rows, LANES), F32),
                        pltpu.VMEM((2, rows, LANES), F32),
                        pltpu.VMEM((heads, seq + tile, LANES), F32)],
        out_shape=jax.ShapeDtypeStruct((bsz, seq, aw), BF16),
        compiler_params=pltpu.CompilerParams(
            dimension_semantics=("arbitrary", "arbitrary"),
            vmem_limit_bytes=VMEM_LIMIT_BYTES),
        name="stickbreak_attn",
    )(qn2, kn2, q, k, v, u, cm, _group_mean_matrix(LANES, heads), g_attn.reshape(1, aw))


def _layer_norm(x, g, b):
    mu = jnp.mean(x, axis=-1, keepdims=True)
    xc = x - mu
    var = jnp.mean(xc * xc, axis=-1, keepdims=True)
    return xc * lax.rsqrt(var + LN_EPS) * g + b


def _mlp_kernel(x_ref, yc_ref, ya_ref, wo_ref, g1_ref, b1_ref, wu_ref, wd_ref,
                g2_ref, b2_ref, o_ref, *, cw, alpha, chunk):
    blocks = [slice(r, r + MLP_SUB_ROWS) for r in range(0, x_ref.shape[0], MLP_SUB_ROWS)]
    mix = [_dot(yc_ref[b, :], wo_ref[0:cw, :]) + _dot(ya_ref[b, :], wo_ref[cw:, :])
           for b in blocks]
    for b, mix_b in zip(blocks, mix):
        x1 = _layer_norm(alpha * x_ref[b, :] + mix_b, g1_ref[...], b1_ref[...])
        x1b = x1.astype(BF16)
        ffn = None
        for c0 in range(0, wu_ref.shape[1], chunk):
            hid = jnp.maximum(_dot(x1b, wu_ref[:, c0:c0 + chunk]), 0.0)
            part = _dot((hid * hid).astype(BF16), wd_ref[c0:c0 + chunk, :])
            ffn = part if ffn is None else ffn + part
        o_ref[b, :] = _layer_norm(alpha * x1 + ffn, g2_ref[...], b2_ref[...])


def _mlp_call(x2, yc2, ya2, w_out, ln1_g, ln1_b, w_up, w_down, ln2_g, ln2_b, alpha):
    t, d = x2.shape
    cw = yc2.shape[1]
    aw = ya2.shape[1]
    rows = MLP_ROWS
    kern = functools.partial(_mlp_kernel, cw=cw, alpha=alpha, chunk=MLP_CHUNK)

    def const(shape):
        return pl.BlockSpec(shape, lambda i: (0, 0), pipeline_mode=pl.Buffered(1))

    return pl.pallas_call(
        kern,
        grid=(t // rows,),
        in_specs=[
            pl.BlockSpec((rows, d), lambda i: (i, 0)),
            pl.BlockSpec((rows, cw), lambda i: (i, 0)),
            pl.BlockSpec((rows, aw), lambda i: (i, 0)),
            const(w_out.shape), const((1, d)), const((1, d)),
            const(w_up.shape), const(w_down.shape), const((1, d)), const((1, d)),
        ],
        out_specs=pl.BlockSpec((rows, d), lambda i: (i, 0)),
        out_shape=jax.ShapeDtypeStruct((t, d), F32),
        compiler_params=pltpu.CompilerParams(
            dimension_semantics=("arbitrary",),
            vmem_limit_bytes=VMEM_LIMIT_BYTES),
        name="outproj_mlp",
    )(x2, yc2, ya2, w_out.astype(BF16), ln1_g.reshape(1, d), ln1_b.reshape(1, d),
      w_up.astype(BF16), w_down.astype(BF16), ln2_g.reshape(1, d), ln2_b.reshape(1, d))


def kernel(x, w_in, conv_w, g_conv, g_attn, w_out, ln1_g, ln1_b, w_up, w_down, ln2_g, ln2_b):
    bsz, seq, d = x.shape
    depth = w_in.shape[0]
    cw = conv_w.shape[2]
    aw = g_attn.shape[1]
    assert w_in.shape[2] == 3 * cw + 3 * aw and w_out.shape[1] == cw + aw
    assert aw % LANES == 0 and seq % PROJ_ROWS == 0 and seq % NORM_ROWS == 0
    assert seq % ATTN_TILE == 0 and ATTN_TILE % LANES == 0
    alpha = float((2 * depth) ** 0.25)
    for l in range(depth):
        yc, q, k, v, qn2, kn2 = _proj_call(x, w_in[l], conv_w[l], g_conv[l], cw, aw)
        ya = _attn_call(q, k, v, qn2, kn2, g_attn[l:l + 1])
        x = _mlp_call(x.reshape(bsz * seq, d), yc.reshape(bsz * seq, cw),
                      ya.reshape(bsz * seq, aw), w_out[l], ln1_g[l], ln1_b[l],
                      w_up[l], w_down[l], ln2_g[l], ln2_b[l], alpha
                      ).reshape(bsz, seq, d)
    return x
```

```python
import functools

import jax
import jax.numpy as jnp
from jax import lax
from jax.experimental import pallas as pl
from jax.experimental.pallas import tpu as pltpu

F32 = jnp.float32
BF16 = jnp.bfloat16

CONV_GROUPS = 8
CONV_K = 3
ATTN_HEADS = 8
LN_EPS = 1e-5
RMS_EPS = 1e-6
LOGIT_CAP = 40.0
MASKED_LOGIT = -1e4
UNDERFLOW_MARGIN = 110.0
LOGIT_BOUND_SLACK = 1.01

LANES = 128
VMEM_LIMIT_BYTES = 56 * 1024 * 1024

PROJ_ROWS = 1024
PROJ_SUB_ROWS = 256
ATTN_TILE = 256
ATTN_UNROLL = 6
NORM_ROWS = 1024
MLP_ROWS = 1024
MLP_SUB_ROWS = 256
MLP_CHUNK = 1024


def _dot(a, b):
    return jnp.dot(a, b, preferred_element_type=F32)


def _group_mean_square(y, gm):
    return _dot((y * y).astype(BF16), gm)


def _max_head_norm2(t, head_dim):
    n_heads = t.shape[1] // head_dim
    per_block = LANES // head_dim
    tf = t.astype(F32)
    t2 = tf * tf
    lane = lax.broadcasted_iota(jnp.int32, (1, LANES), 1)
    out_row = lax.broadcasted_iota(jnp.int32, (n_heads, LANES), 0)
    out = jnp.zeros((n_heads, LANES), F32)
    for c in range(t.shape[1] // LANES):
        blk = t2[:, c * LANES:(c + 1) * LANES]
        for j in range(per_block):
            in_head = (lane >= j * head_dim) & (lane < (j + 1) * head_dim)
            norm2 = jnp.sum(jnp.where(in_head, blk, 0.0), axis=-1, keepdims=True)
            out = jnp.where(out_row == c * per_block + j,
                            jnp.max(norm2, axis=0, keepdims=True), out)
    return out


def _proj_kernel(x_ref, w_ref, cw_ref, g_ref, gm_ref,
                 yc_ref, q_ref, k_ref, v_ref, qn_ref, kn_ref, u_ref,
                 *, rows, cw, aw, q_scale, head_dim):
    i = pl.program_id(1)
    sub = PROJ_SUB_ROWS

    @pl.when(i == 0)
    def _():
        u_ref[0:8, :] = jnp.zeros((8, cw), F32)
        qn_ref[...] = jnp.zeros_like(qn_ref)
        kn_ref[...] = jnp.zeros_like(kn_ref)

    @pl.when(i > 0)
    def _():
        u_ref[0:8, :] = u_ref[rows:rows + 8, :]

    qn = qn_ref[...]
    kn = kn_ref[...]
    gates = []
    for r0 in range(0, rows, sub):
        xb = x_ref[r0:r0 + sub, :].astype(BF16)

        def proj(c0, width, xb=xb):
            return _dot(xb, w_ref[:, c0:c0 + width])

        qb = (proj(3 * cw, aw) * q_scale).astype(BF16)
        kb = proj(3 * cw + aw, aw).astype(BF16)
        q_ref[r0:r0 + sub, :] = qb
        k_ref[r0:r0 + sub, :] = kb
        v_ref[r0:r0 + sub, :] = proj(3 * cw + 2 * aw, aw).astype(BF16)
        qn = jnp.maximum(qn, _max_head_norm2(qb, head_dim))
        kn = jnp.maximum(kn, _max_head_norm2(kb, head_dim))
        b_gate = proj(0, cw)
        u = proj(cw, cw) * proj(2 * cw, cw)
        u_ref[8 + r0:8 + r0 + sub, :] = u
        gates.append((b_gate, u))
    qn_ref[...] = qn
    kn_ref[...] = kn

    for r0, (b_gate, u) in zip(range(0, rows, sub), gates):
        u1 = u_ref[7 + r0:7 + r0 + sub, :]
        u2 = u_ref[6 + r0:6 + r0 + sub, :]
        y = b_gate * (cw_ref[0:1, :] * u2 + cw_ref[1:2, :] * u1 + cw_ref[2:3, :] * u)
        ms = _group_mean_square(y, gm_ref[...])
        yc_ref[r0:r0 + sub, :] = (y * lax.rsqrt(ms + RMS_EPS) * g_ref[...]).astype(BF16)


def _group_mean_matrix(width, groups):
    gid = jnp.arange(width) // (width // groups)
    return jnp.where(gid[:, None] == gid[None, :], groups / width, 0.0).astype(BF16)


def _proj_call(x, w_in, conv_w, g_conv, cw, aw):
    bsz, seq, d = x.shape
    rows = PROJ_ROWS
    head_dim = aw // ATTN_HEADS
    kern = functools.partial(_proj_kernel, rows=rows, cw=cw, aw=aw,
                             q_scale=head_dim ** -0.5, head_dim=head_dim)
    out_spec = pl.BlockSpec((None, rows, cw), lambda b, i: (b, i, 0))
    out_spec_a = pl.BlockSpec((None, rows, aw), lambda b, i: (b, i, 0))
    norm_spec = pl.BlockSpec((None, ATTN_HEADS, LANES), lambda b, i: (b, 0, 0))
    return pl.pallas_call(
        kern,
        grid=(bsz, seq // rows),
        in_specs=[
            pl.BlockSpec((None, rows, d), lambda b, i: (b, i, 0)),
            pl.BlockSpec(w_in.shape, lambda b, i: (0, 0)),
            pl.BlockSpec(conv_w.shape, lambda b, i: (0, 0)),
            pl.BlockSpec((1, cw), lambda b, i: (0, 0)),
            pl.BlockSpec((cw, cw), lambda b, i: (0, 0)),
        ],
        out_specs=[out_spec, out_spec_a, out_spec_a, out_spec_a, norm_spec, norm_spec],
        out_shape=[jax.ShapeDtypeStruct((bsz, seq, cw), BF16)]
        + [jax.ShapeDtypeStruct((bsz, seq, aw), BF16)] * 3
        + [jax.ShapeDtypeStruct((bsz, ATTN_HEADS, LANES), F32)] * 2,
        scratch_shapes=[pltpu.VMEM((rows + 8, cw), F32)],
        compiler_params=pltpu.CompilerParams(
            dimension_semantics=("arbitrary", "arbitrary"),
            vmem_limit_bytes=VMEM_LIMIT_BYTES),
        name="proj_conv",
    )(x, w_in.astype(BF16), conv_w, g_conv.reshape(1, cw),
      _group_mean_matrix(cw, CONV_GROUPS))


def _attn_kernel(qn_ref, kn_ref, q_ref, k_ref, v_ref, u_ref, cm_ref, gm_ref, g_ref, o_ref,
                 qm_ref, z_ref, hl_ref, lbc_ref, a_ref, carry_ref, acc_ref, oraw_ref,
                 *, tile, head_dim):
    heads = LANES // head_dim
    seq = q_ref.shape[0]
    n_q = seq // tile
    thr = (jnp.max(jnp.sqrt(qn_ref[...] * kn_ref[...])) * LOGIT_BOUND_SLACK
           + UNDERFLOW_MARGIN)
    lane = lax.broadcasted_iota(jnp.int32, (1, LANES), 1)
    in_head = [(lane >= h * head_dim) & (lane < (h + 1) * head_dim)
               for h in range(heads)]
    def mask_heads(t, c):
        r = pl.multiple_of(t * tile, tile)
        q = q_ref[pl.ds(r, tile), :]
        for h in range(heads):
            qm_ref[h, pl.ds(r, tile), :] = jnp.where(in_head[h], q, jnp.zeros_like(q))
        return c

    lax.fori_loop(0, n_q, mask_heads, 0)
    carry_ref[...] = jnp.zeros_like(carry_ref)
    acc_ref[...] = jnp.zeros_like(acc_ref)
    a_ref[...] = jnp.zeros_like(a_ref)
    hl_ref[1] = jnp.zeros(hl_ref.shape[1:], BF16)
    lbc_ref[1] = jnp.full(lbc_ref.shape[1:], MASKED_LOGIT, F32)
    z_ref[1] = jnp.full(z_ref.shape[1:], MASKED_LOGIT, F32)

    def row_start(t):
        return pl.multiple_of(jnp.clip(t, 0, n_q - 1) * tile, tile)

    def next_tile(tq, tk, rest_is_zero):
        new_sweep = (tk == 0) | rest_is_zero
        nq = jnp.minimum(jnp.where(new_sweep, tq + 2, tq), n_q)
        nk = jnp.where(nq >= n_q, n_q, jnp.where(new_sweep, nq, tk - 1))
        return nq, nk

    def stage0(tq, tk, p):
        qm = jnp.concatenate([qm_ref[h, pl.ds(row_start(tq), tile), :]
                              for h in range(heads)], axis=0)
        kt = k_ref[pl.ds(row_start(tk), tile), :]
        z = lax.dot_general(qm, kt, (((1,), (1,)), ((), ())), preferred_element_type=F32)
        z_ref[p] = jnp.minimum(z, cm_ref[(tq == tk).astype(jnp.int32)])

    def stage1(tq, tk, p):
        z = z_ref[1 - p]
        sp = jnp.maximum(z, jnp.log(1.0 + jnp.exp(z)))
        carry = jnp.where(tq == tk, 0.0, carry_ref[1 - p])
        lbc_ref[p] = z - jnp.concatenate([carry] * (tile // LANES), axis=1)
        hl_ref[p] = sp.astype(BF16)
        carry = carry + jnp.sum(sp, axis=-1, keepdims=True)
        carry_ref[1 - p] = carry
        return jnp.min(carry) > thr

    def stage2(p):
        suffix = _dot(hl_ref[1 - p], u_ref[...])
        a_ref[1 - p] = jnp.exp(lbc_ref[1 - p] - suffix).astype(BF16)

    def stage3(tq, tk, p):
        vt = v_ref[pl.ds(row_start(tk), tile), :]
        acc = jnp.where(tq == tk, 0.0, acc_ref[1 - p]) + _dot(a_ref[p], vt)
        acc_ref[1 - p] = acc
        idle = (tq < 0) | (tq >= n_q)
        r3 = pl.multiple_of(jnp.where(idle, n_q, tq) * tile, tile)
        for h in range(heads):
            oraw_ref[h, pl.ds(r3, tile), :] = acc[h * tile:(h + 1) * tile]

    def step(state, p):
        q1, k1, q2, k2, q3, k3, rest_is_zero, _ = state
        q0, k0 = next_tile(q2, k2, rest_is_zero)
        other_rest_is_zero = stage1(q1, k1, p)
        stage2(p)
        stage3(q3, k3, p)
        stage0(q0, k0, p)
        flushed = (q0 >= n_q) & (q1 >= n_q) & (q2 >= n_q)
        return (q0, k0, q1, k1, q2, k2, other_rest_is_zero, flushed)

    def body(state):
        for j in range(ATTN_UNROLL):
            state = step(state, j % 2)
        return state

    def not_started(lane):
        return (jnp.int32(lane - 2), jnp.int32(0))

    lax.while_loop(lambda st: jnp.logical_not(st[-1]), body,
                   not_started(1) + not_started(0) + not_started(1)
                   + (jnp.bool_(False), jnp.bool_(False)))

    def normalise(t, c):
        r = pl.multiple_of(t * NORM_ROWS, NORM_ROWS)
        o = jnp.where(in_head[0], oraw_ref[0, pl.ds(r, NORM_ROWS), :], 0.0)
        for h in range(1, heads):
            o = o + jnp.where(in_head[h], oraw_ref[h, pl.ds(r, NORM_ROWS), :], 0.0)
        ms = _group_mean_square(o, gm_ref[...])
        o_ref[pl.ds(r, NORM_ROWS), :] = (o * lax.rsqrt(ms + RMS_EPS) * g_ref[...]).astype(BF16)
        return c

    lax.fori_loop(0, seq // NORM_ROWS, normalise, 0)


def _attn_call(q, k, v, qn2, kn2, g_attn):
    bsz, seq, aw = q.shape
    tile = ATTN_TILE
    head_dim = aw // ATTN_HEADS
    heads = LANES // head_dim
    rows = heads * tile
    pairs = aw // LANES
    qn2 = qn2.reshape(bsz, pairs, heads, LANES)
    kn2 = kn2.reshape(bsz, pairs, heads, LANES)
    norm_spec = pl.BlockSpec((None, None, heads, LANES), lambda b, p: (b, p, 0, 0))
    kern = functools.partial(_attn_kernel, tile=tile, head_dim=head_dim)
    jj = jnp.arange(tile)
    u = (jj[:, None] >= jj[None, :]).astype(BF16)
    qpos = jnp.arange(rows)[:, None] % tile
    cm = jnp.stack([jnp.full((rows, tile), LOGIT_CAP, F32),
                    jnp.where(jj[None, :] < qpos, LOGIT_CAP, MASKED_LOGIT).astype(F32)])
    seq_spec = pl.BlockSpec((None, seq, LANES), lambda b, p: (b, 0, p))

    def const(shape):
        return pl.BlockSpec(shape, lambda b, p: (0,) * len(shape))

    return pl.pallas_call(
        kern,
        grid=(bsz, pairs),
        in_specs=[norm_spec, norm_spec, seq_spec, seq_spec, seq_spec,
                  const((tile, tile)), const((2, rows, tile)),
                  const((LANES, LANES)),
                  pl.BlockSpec((1, LANES), lambda b, p: (0, p))],
        out_specs=seq_spec,
        scratch_shapes=[pltpu.VMEM((heads, seq, LANES), BF16),
                        pltpu.VMEM((2, rows, tile), F32),
                        pltpu.VMEM((2, rows, tile), BF16),
                        pltpu.VMEM((2, rows, tile), F32),
                        pltpu.VMEM((2, rows, tile), BF16),
                        pltpu.VMEM((2, rows, LANES), F32),
                        pltpu.VMEM((2, rows, LANES), F32),
                        pltpu.VMEM((heads, seq + tile, LANES), F32)],
        out_shape=jax.ShapeDtypeStruct((bsz, seq, aw), BF16),
        compiler_params=pltpu.CompilerParams(
            dimension_semantics=("arbitrary", "arbitrary"),
            vmem_limit_bytes=VMEM_LIMIT_BYTES),
        name="stickbreak_attn",
    )(qn2, kn2, q, k, v, u, cm, _group_mean_matrix(LANES, heads), g_attn.reshape(1, aw))


def _layer_norm(x, g, b):
    mu = jnp.mean(x, axis=-1, keepdims=True)
    xc = x - mu
    var = jnp.mean(xc * xc, axis=-1, keepdims=True)
    return xc * lax.rsqrt(var + LN_EPS) * g + b


def _mlp_kernel(x_ref, yc_ref, ya_ref, wo_ref, g1_ref, b1_ref, wu_ref, wd_ref,
                g2_ref, b2_ref, o_ref, *, cw, alpha, chunk):
    blocks = [slice(r, r + MLP_SUB_ROWS) for r in range(0, x_ref.shape[0], MLP_SUB_ROWS)]
    mix = [_dot(yc_ref[b, :], wo_ref[0:cw, :]) + _dot(ya_ref[b, :], wo_ref[cw:, :])
           for b in blocks]
    for b, mix_b in zip(blocks, mix):
        x1 = _layer_norm(alpha * x_ref[b, :] + mix_b, g1_ref[...], b1_ref[...])
        x1b = x1.astype(BF16)
        ffn = None
        for c0 in range(0, wu_ref.shape[1], chunk):
            hid = jnp.maximum(_dot(x1b, wu_ref[:, c0:c0 + chunk]), 0.0)
            part = _dot((hid * hid).astype(BF16), wd_ref[c0:c0 + chunk, :])
            ffn = part if ffn is None else ffn + part
        o_ref[b, :] = _layer_norm(alpha * x1 + ffn, g2_ref[...], b2_ref[...])


def _mlp_call(x2, yc2, ya2, w_out, ln1_g, ln1_b, w_up, w_down, ln2_g, ln2_b, alpha):
    t, d = x2.shape
    cw = yc2.shape[1]
    aw = ya2.shape[1]
    rows = MLP_ROWS
    kern = functools.partial(_mlp_kernel, cw=cw, alpha=alpha, chunk=MLP_CHUNK)

    def const(shape):
        return pl.BlockSpec(shape, lambda i: (0, 0), pipeline_mode=pl.Buffered(1))

    return pl.pallas_call(
        kern,
        grid=(t // rows,),
        in_specs=[
            pl.BlockSpec((rows, d), lambda i: (i, 0)),
            pl.BlockSpec((rows, cw), lambda i: (i, 0)),
            pl.BlockSpec((rows, aw), lambda i: (i, 0)),
            const(w_out.shape), const((1, d)), const((1, d)),
            const(w_up.shape), const(w_down.shape), const((1, d)), const((1, d)),
        ],
        out_specs=pl.BlockSpec((rows, d), lambda i: (i, 0)),
        out_shape=jax.ShapeDtypeStruct((t, d), F32),
        compiler_params=pltpu.CompilerParams(
            dimension_semantics=("arbitrary",),
            vmem_limit_bytes=VMEM_LIMIT_BYTES),
        name="outproj_mlp",
    )(x2, yc2, ya2, w_out.astype(BF16), ln1_g.reshape(1, d), ln1_b.reshape(1, d),
      w_up.astype(BF16), w_down.astype(BF16), ln2_g.reshape(1, d), ln2_b.reshape(1, d))


def kernel(x, w_in, conv_w, g_conv, g_attn, w_out, ln1_g, ln1_b, w_up, w_down, ln2_g, ln2_b):
    bsz, seq, d = x.shape
    depth = w_in.shape[0]
    cw = conv_w.shape[2]
    aw = g_attn.shape[1]
    assert w_in.shape[2] == 3 * cw + 3 * aw and w_out.shape[1] == cw + aw
    assert aw % LANES == 0 and seq % PROJ_ROWS == 0 and seq % NORM_ROWS == 0
    assert seq % ATTN_TILE == 0 and ATTN_TILE % LANES == 0
    alpha = float((2 * depth) ** 0.25)
    for l in range(depth):
        yc, q, k, v, qn2, kn2 = _proj_call(x, w_in[l], conv_w[l], g_conv[l], cw, aw)
        ya = _attn_call(q, k, v, qn2, kn2, g_attn[l:l + 1])
        x = _mlp_call(x.reshape(bsz * seq, d), yc.reshape(bsz * seq, cw),
                      ya.reshape(bsz * seq, aw), w_out[l], ln1_g[l], ln1_b[l],
                      w_up[l], w_down[l], ln2_g[l], ln2_b[l], alpha
                      ).reshape(bsz, seq, d)
    return x
```

```python
import functools

import jax
import jax.numpy as jnp
from jax import lax
from jax.experimental import pallas as pl
from jax.experimental.pallas import tpu as pltpu

F32 = jnp.float32
BF16 = jnp.bfloat16

CONV_GROUPS = 8
CONV_K = 3
ATTN_HEADS = 8
LN_EPS = 1e-5
RMS_EPS = 1e-6
LOGIT_CAP = 40.0
MASKED_LOGIT = -1e4
UNDERFLOW_MARGIN = 110.0
LOGIT_BOUND_SLACK = 1.01

LANES = 128
VMEM_LIMIT_BYTES = 56 * 1024 * 1024

PROJ_ROWS = 1024
PROJ_SUB_ROWS = 256
ATTN_TILE = 256
ATTN_UNROLL = 6
NORM_ROWS = 1024
MLP_ROWS = 1024
MLP_SUB_ROWS = 256
MLP_CHUNK = 1024


def _dot(a, b):
    return jnp.dot(a, b, preferred_element_type=F32)


def _group_mean_square(y, gm):
    return _dot((y * y).astype(BF16), gm)


def _max_head_norm2(t, head_dim):
    n_heads = t.shape[1] // head_dim
    per_block = LANES // head_dim
    tf = t.astype(F32)
    t2 = tf * tf
    lane = lax.broadcasted_iota(jnp.int32, (1, LANES), 1)
    out_row = lax.broadcasted_iota(jnp.int32, (n_heads, LANES), 0)
    out = jnp.zeros((n_heads, LANES), F32)
    for c in range(t.shape[1] // LANES):
        blk = t2[:, c * LANES:(c + 1) * LANES]
        for j in range(per_block):
            in_head = (lane >= j * head_dim) & (lane < (j + 1) * head_dim)
            norm2 = jnp.sum(jnp.where(in_head, blk, 0.0), axis=-1, keepdims=True)
            out = jnp.where(out_row == c * per_block + j,
                            jnp.max(norm2, axis=0, keepdims=True), out)
    return out


def _proj_kernel(x_ref, w32_ref, cw_ref, g_ref, gm_ref, wo32_ref, wu32_ref, wd32_ref,
                 yc_ref, q_ref, k_ref, v_ref, qn_ref, kn_ref, wo_ref, wu_ref, wd_ref,
                 u_ref, w_ref, *, rows, cw, aw, q_scale, head_dim):
    i = pl.program_id(1)
    sub = PROJ_SUB_ROWS

    @pl.when((pl.program_id(0) == 0) & (i == 0))
    def _():
        def cast_rows(t, c):
            r = pl.multiple_of(t * LANES, LANES)
            w_ref[pl.ds(r, LANES), :] = w32_ref[pl.ds(r, LANES), :].astype(BF16)
            return c
        lax.fori_loop(0, w_ref.shape[0] // LANES, cast_rows, 0)

    wo_ref[...] = wo32_ref[...].astype(BF16)
    wu_ref[...] = wu32_ref[...].astype(BF16)
    wd_ref[...] = wd32_ref[...].astype(BF16)

    @pl.when(i == 0)
    def _():
        u_ref[0:8, :] = jnp.zeros((8, cw), F32)
        qn_ref[...] = jnp.zeros_like(qn_ref)
        kn_ref[...] = jnp.zeros_like(kn_ref)

    @pl.when(i > 0)
    def _():
        u_ref[0:8, :] = u_ref[rows:rows + 8, :]

    qn = qn_ref[...]
    kn = kn_ref[...]
    gates = []
    for r0 in range(0, rows, sub):
        xb = x_ref[r0:r0 + sub, :].astype(BF16)

        def proj(c0, width, xb=xb):
            return _dot(xb, w_ref[:, c0:c0 + width])

        qb = (proj(3 * cw, aw) * q_scale).astype(BF16)
        kb = proj(3 * cw + aw, aw).astype(BF16)
        q_ref[r0:r0 + sub, :] = qb
        k_ref[r0:r0 + sub, :] = kb
        v_ref[r0:r0 + sub, :] = proj(3 * cw + 2 * aw, aw).astype(BF16)
        qn = jnp.maximum(qn, _max_head_norm2(qb, head_dim))
        kn = jnp.maximum(kn, _max_head_norm2(kb, head_dim))
        b_gate = proj(0, cw)
        u = proj(cw, cw) * proj(2 * cw, cw)
        u_ref[8 + r0:8 + r0 + sub, :] = u
        gates.append((b_gate, u))
    qn_ref[...] = qn
    kn_ref[...] = kn

    for r0, (b_gate, u) in zip(range(0, rows, sub), gates):
        u1 = u_ref[7 + r0:7 + r0 + sub, :]
        u2 = u_ref[6 + r0:6 + r0 + sub, :]
        y = b_gate * (cw_ref[0:1, :] * u2 + cw_ref[1:2, :] * u1 + cw_ref[2:3, :] * u)
        ms = _group_mean_square(y, gm_ref[...])
        yc_ref[r0:r0 + sub, :] = (y * lax.rsqrt(ms + RMS_EPS) * g_ref[...]).astype(BF16)


def _group_mean_matrix(width, groups):
    gid = jnp.arange(width) // (width // groups)
    return jnp.where(gid[:, None] == gid[None, :], groups / width, 0.0).astype(BF16)


def _proj_call(x, w_in, conv_w, g_conv, w_out, w_up, w_down, cw, aw):
    bsz, seq, d = x.shape
    rows = PROJ_ROWS
    head_dim = aw // ATTN_HEADS
    n_i = seq // rows
    n_steps = bsz * n_i
    d_ff = w_up.shape[1]
    assert w_in.shape[0] % LANES == 0
    assert w_out.shape[0] % (16 * n_steps) == 0 and d_ff % (LANES * n_steps) == 0
    wo_spec = pl.BlockSpec((w_out.shape[0] // n_steps, w_out.shape[1]),
                           lambda b, i: (b * n_i + i, 0))
    wu_spec = pl.BlockSpec((d, d_ff // n_steps), lambda b, i: (0, b * n_i + i))
    wd_spec = pl.BlockSpec((d_ff // n_steps, d), lambda b, i: (b * n_i + i, 0))
    kern = functools.partial(_proj_kernel, rows=rows, cw=cw, aw=aw,
                             q_scale=head_dim ** -0.5, head_dim=head_dim)
    out_spec = pl.BlockSpec((None, rows, cw), lambda b, i: (b, i, 0))
    out_spec_a = pl.BlockSpec((None, rows, aw), lambda b, i: (b, i, 0))
    norm_spec = pl.BlockSpec((None, ATTN_HEADS, LANES), lambda b, i: (b, 0, 0))
    return pl.pallas_call(
        kern,
        grid=(bsz, n_i),
        in_specs=[
            pl.BlockSpec((None, rows, d), lambda b, i: (b, i, 0)),
            pl.BlockSpec(w_in.shape, lambda b, i: (0, 0), pipeline_mode=pl.Buffered(1)),
            pl.BlockSpec(conv_w.shape, lambda b, i: (0, 0)),
            pl.BlockSpec((1, cw), lambda b, i: (0, 0)),
            pl.BlockSpec((cw, cw), lambda b, i: (0, 0)),
            wo_spec, wu_spec, wd_spec,
        ],
        out_specs=[out_spec, out_spec_a, out_spec_a, out_spec_a, norm_spec, norm_spec,
                   wo_spec, wu_spec, wd_spec],
        out_shape=[jax.ShapeDtypeStruct((bsz, seq, cw), BF16)]
        + [jax.ShapeDtypeStruct((bsz, seq, aw), BF16)] * 3
        + [jax.ShapeDtypeStruct((bsz, ATTN_HEADS, LANES), F32)] * 2
        + [jax.ShapeDtypeStruct(w.shape, BF16) for w in (w_out, w_up, w_down)],
        scratch_shapes=[pltpu.VMEM((rows + 8, cw), F32),
                        pltpu.VMEM(w_in.shape, BF16)],
        compiler_params=pltpu.CompilerParams(
            dimension_semantics=("arbitrary", "arbitrary"),
            vmem_limit_bytes=VMEM_LIMIT_BYTES),
        name="proj_conv",
    )(x, w_in, conv_w, g_conv.reshape(1, cw), _group_mean_matrix(cw, CONV_GROUPS),
      w_out, w_up, w_down)


def _attn_kernel(qn_ref, kn_ref, q_ref, k_ref, v_ref, u_ref, cm_ref, gm_ref, g_ref, o_ref,
                 qm_ref, z_ref, hl_ref, lbc_ref, a_ref, carry_ref, acc_ref, oraw_ref,
                 *, tile, head_dim):
    heads = LANES // head_dim
    seq = q_ref.shape[0]
    n_q = seq // tile
    thr = (jnp.max(jnp.sqrt(qn_ref[...] * kn_ref[...])) * LOGIT_BOUND_SLACK
           + UNDERFLOW_MARGIN)
    lane = lax.broadcasted_iota(jnp.int32, (1, LANES), 1)
    in_head = [(lane >= h * head_dim) & (lane < (h + 1) * head_dim)
               for h in range(heads)]
    def mask_heads(t, c):
        r = pl.multiple_of(t * tile, tile)
        q = q_ref[pl.ds(r, tile), :]
        for h in range(heads):
            qm_ref[h, pl.ds(r, tile), :] = jnp.where(in_head[h], q, jnp.zeros_like(q))
        return c

    lax.fori_loop(0, n_q, mask_heads, 0)
    carry_ref[...] = jnp.zeros_like(carry_ref)
    acc_ref[...] = jnp.zeros_like(acc_ref)
    a_ref[...] = jnp.zeros_like(a_ref)
    hl_ref[1] = jnp.zeros(hl_ref.shape[1:], BF16)
    lbc_ref[1] = jnp.full(lbc_ref.shape[1:], MASKED_LOGIT, F32)
    z_ref[1] = jnp.full(z_ref.shape[1:], MASKED_LOGIT, F32)

    def row_start(t):
        return pl.multiple_of(jnp.clip(t, 0, n_q - 1) * tile, tile)

    def next_tile(tq, tk, rest_is_zero):
        new_sweep = (tk == 0) | rest_is_zero
        nq = jnp.minimum(jnp.where(new_sweep, tq + 2, tq), n_q)
        nk = jnp.where(nq >= n_q, n_q, jnp.where(new_sweep, nq, tk - 1))
        return nq, nk

    def stage0(tq, tk, p):
        qm = jnp.concatenate([qm_ref[h, pl.ds(row_start(tq), tile), :]
                              for h in range(heads)], axis=0)
        kt = k_ref[pl.ds(row_start(tk), tile), :]
        z = lax.dot_general(qm, kt, (((1,), (1,)), ((), ())), preferred_element_type=F32)
        z_ref[p] = jnp.minimum(z, cm_ref[(tq == tk).astype(jnp.int32)])

    def stage1(tq, tk, p):
        z = z_ref[1 - p]
        sp = jnp.maximum(z, jnp.log(1.0 + jnp.exp(z)))
        carry = jnp.where(tq == tk, 0.0, carry_ref[1 - p])
        lbc_ref[p] = z - jnp.concatenate([carry] * (tile // LANES), axis=1)
        hl_ref[p] = sp.astype(BF16)
        carry = carry + jnp.sum(sp, axis=-1, keepdims=True)
        carry_ref[1 - p] = carry
        return jnp.min(carry) > thr

    def stage2(p):
        suffix = _dot(hl_ref[1 - p], u_ref[...])
        a_ref[1 - p] = jnp.exp(lbc_ref[1 - p] - suffix).astype(BF16)

    def stage3(tq, tk, p):
        vt = v_ref[pl.ds(row_start(tk), tile), :]
        acc = jnp.where(tq == tk, 0.0, acc_ref[1 - p]) + _dot(a_ref[p], vt)
        acc_ref[1 - p] = acc
        idle = (tq < 0) | (tq >= n_q)
        r3 = pl.multiple_of(jnp.where(idle, n_q, tq) * tile, tile)
        for h in range(heads):
            oraw_ref[h, pl.ds(r3, tile), :] = acc[h * tile:(h + 1) * tile]

    def step(state, p):
        q1, k1, q2, k2, q3, k3, rest_is_zero, _ = state
        q0, k0 = next_tile(q2, k2, rest_is_zero)
        other_rest_is_zero = stage1(q1, k1, p)
        stage2(p)
        stage3(q3, k3, p)
        stage0(q0, k0, p)
        flushed = (q0 >= n_q) & (q1 >= n_q) & (q2 >= n_q)
        return (q0, k0, q1, k1, q2, k2, other_rest_is_zero, flushed)

    def body(state):
        for j in range(ATTN_UNROLL):
            state = step(state, j % 2)
        return state

    def not_started(lane):
        return (jnp.int32(lane - 2), jnp.int32(0))

    lax.while_loop(lambda st: jnp.logical_not(st[-1]), body,
                   not_started(1) + not_started(0) + not_started(1)
                   + (jnp.bool_(False), jnp.bool_(False)))

    def normalise(t, c):
        r = pl.multiple_of(t * NORM_ROWS, NORM_ROWS)
        o = jnp.where(in_head[0], oraw_ref[0, pl.ds(r, NORM_ROWS), :], 0.0)
        for h in range(1, heads):
            o = o + jnp.where(in_head[h], oraw_ref[h, pl.ds(r, NORM_ROWS), :], 0.0)
        ms = _group_mean_square(o, gm_ref[...])
        o_ref[pl.ds(r, NORM_ROWS), :] = (o * lax.rsqrt(ms + RMS_EPS) * g_ref[...]).astype(BF16)
        return c

    lax.fori_loop(0, seq // NORM_ROWS, normalise, 0)


def _attn_call(q, k, v, qn2, kn2, g_attn):
    bsz, seq, aw = q.shape
    tile = ATTN_TILE
    head_dim = aw // ATTN_HEADS
    heads = LANES // head_dim
    rows = heads * tile
    pairs = aw // LANES
    qn2 = qn2.reshape(bsz, pairs, heads, LANES)
    kn2 = kn2.reshape(bsz, pairs, heads, LANES)
    norm_spec = pl.BlockSpec((None, None, heads, LANES), lambda b, p: (b, p, 0, 0))
    kern = functools.partial(_attn_kernel, tile=tile, head_dim=head_dim)
    jj = jnp.arange(tile)
    u = (jj[:, None] >= jj[None, :]).astype(BF16)
    qpos = jnp.arange(rows)[:, None] % tile
    cm = jnp.stack([jnp.full((rows, tile), LOGIT_CAP, F32),
                    jnp.where(jj[None, :] < qpos, LOGIT_CAP, MASKED_LOGIT).astype(F32)])
    seq_spec = pl.BlockSpec((None, seq, LANES), lambda b, p: (b, 0, p))

    def const(shape):
        return pl.BlockSpec(shape, lambda b, p: (0,) * len(shape))

    return pl.pallas_call(
        kern,
        grid=(bsz, pairs),
        in_specs=[norm_spec, norm_spec, seq_spec, seq_spec, seq_spec,
                  const((tile, tile)), const((2, rows, tile)),
                  const((LANES, LANES)),
                  pl.BlockSpec((1, LANES), lambda b, p: (0, p))],
        out_specs=seq_spec,
        scratch_shapes=[pltpu.VMEM((heads, seq, LANES), BF16),
                        pltpu.VMEM((2, rows, tile), F32),
                        pltpu.VMEM((2, rows, tile), BF16),
                        pltpu.VMEM((2, rows, tile), F32),
                        pltpu.VMEM((2, rows, tile), BF16),
                        pltpu.VMEM((2, rows, LANES), F32),
                        pltpu.VMEM((2, rows, LANES), F32),
                        pltpu.VMEM((heads, seq + tile, LANES), F32)],
        out_shape=jax.ShapeDtypeStruct((bsz, seq, aw), BF16),
        compiler_params=pltpu.CompilerParams(
            dimension_semantics=("arbitrary", "arbitrary"),
            vmem_limit_bytes=VMEM_LIMIT_BYTES),
        name="stickbreak_attn",
    )(qn2, kn2, q, k, v, u, cm, _group_mean_matrix(LANES, heads), g_attn.reshape(1, aw))


def _layer_norm(x, g, b):
    mu = jnp.mean(x, axis=-1, keepdims=True)
    xc = x - mu
    var = jnp.mean(xc * xc, axis=-1, keepdims=True)
    return xc * lax.rsqrt(var + LN_EPS) * g + b


def _mlp_kernel(x_ref, yc_ref, ya_ref, wo_ref, g1_ref, b1_ref, wu_ref, wd_ref,
                g2_ref, b2_ref, o_ref, *, cw, alpha, chunk):
    blocks = [slice(r, r + MLP_SUB_ROWS) for r in range(0, x_ref.shape[0], MLP_SUB_ROWS)]
    mix = [_dot(yc_ref[b, :], wo_ref[0:cw, :]) + _dot(ya_ref[b, :], wo_ref[cw:, :])
           for b in blocks]
    for b, mix_b in zip(blocks, mix):
        x1 = _layer_norm(alpha * x_ref[b, :] + mix_b, g1_ref[...], b1_ref[...])
        x1b = x1.astype(BF16)
        ffn = None
        for c0 in range(0, wu_ref.shape[1], chunk):
            hid = jnp.maximum(_dot(x1b, wu_ref[:, c0:c0 + chunk]), 0.0)
            part = _dot((hid * hid).astype(BF16), wd_ref[c0:c0 + chunk, :])
            ffn = part if ffn is None else ffn + part
        o_ref[b, :] = _layer_norm(alpha * x1 + ffn, g2_ref[...], b2_ref[...])


def _mlp_call(x2, yc2, ya2, w_out, ln1_g, ln1_b, w_up, w_down, ln2_g, ln2_b, alpha):
    t, d = x2.shape
    cw = yc2.shape[1]
    aw = ya2.shape[1]
    rows = MLP_ROWS
    kern = functools.partial(_mlp_kernel, cw=cw, alpha=alpha, chunk=MLP_CHUNK)

    def const(shape):
        return pl.BlockSpec(shape, lambda i: (0, 0), pipeline_mode=pl.Buffered(1))

    return pl.pallas_call(
        kern,
        grid=(t // rows,),
        in_specs=[
            pl.BlockSpec((rows, d), lambda i: (i, 0)),
            pl.BlockSpec((rows, cw), lambda i: (i, 0)),
            pl.BlockSpec((rows, aw), lambda i: (i, 0)),
            const(w_out.shape), const((1, d)), const((1, d)),
            const(w_up.shape), const(w_down.shape), const((1, d)), const((1, d)),
        ],
        out_specs=pl.BlockSpec((rows, d), lambda i: (i, 0)),
        out_shape=jax.ShapeDtypeStruct((t, d), F32),
        compiler_params=pltpu.CompilerParams(
            dimension_semantics=("arbitrary",),
            vmem_limit_bytes=VMEM_LIMIT_BYTES),
        name="outproj_mlp",
    )(x2, yc2, ya2, w_out, ln1_g.reshape(1, d), ln1_b.reshape(1, d),
      w_up, w_down, ln2_g.reshape(1, d), ln2_b.reshape(1, d))


def kernel(x, w_in, conv_w, g_conv, g_attn, w_out, ln1_g, ln1_b, w_up, w_down, ln2_g, ln2_b):
    bsz, seq, d = x.shape
    depth = w_in.shape[0]
    cw = conv_w.shape[2]
    aw = g_attn.shape[1]
    assert w_in.shape[2] == 3 * cw + 3 * aw and w_out.shape[1] == cw + aw
    assert aw % LANES == 0 and seq % PROJ_ROWS == 0 and seq % NORM_ROWS == 0
    assert seq % ATTN_TILE == 0 and ATTN_TILE % LANES == 0
    alpha = float((2 * depth) ** 0.25)
    for l in range(depth):
        yc, q, k, v, qn2, kn2, wo, wu, wd = _proj_call(
            x, w_in[l], conv_w[l], g_conv[l], w_out[l], w_up[l], w_down[l], cw, aw)
        ya = _attn_call(q, k, v, qn2, kn2, g_attn[l:l + 1])
        x = _mlp_call(x.reshape(bsz * seq, d), yc.reshape(bsz * seq, cw),
                      ya.reshape(bsz * seq, aw), wo, ln1_g[l], ln1_b[l],
                      wu, wd, ln2_g[l], ln2_b[l], alpha
                      ).reshape(bsz, seq, d)
    return x
```

```python
import functools

import jax
import jax.numpy as jnp
from jax import lax
from jax.experimental import pallas as pl
from jax.experimental.pallas import tpu as pltpu

F32 = jnp.float32
BF16 = jnp.bfloat16

CONV_GROUPS = 8
CONV_K = 3
ATTN_HEADS = 8
LN_EPS = 1e-5
RMS_EPS = 1e-6
LOGIT_CAP = 40.0
MASKED_LOGIT = -1e4
UNDERFLOW_MARGIN = 110.0
LOGIT_BOUND_SLACK = 1.01

LANES = 128
VMEM_LIMIT_BYTES = 56 * 1024 * 1024

PROJ_ROWS = 1024
PROJ_SUB_ROWS = 256
ATTN_TILE = 256
ATTN_UNROLL = 12
NORM_ROWS = 2048
MLP_ROWS = 1024
MLP_SUB_ROWS = 256
MLP_CHUNK = 1024


def _dot(a, b):
    return jnp.dot(a, b, preferred_element_type=F32)


def _group_mean_square(y, gm):
    return _dot((y * y).astype(BF16), gm)


def _max_head_norm2(t, head_dim):
    n_heads = t.shape[1] // head_dim
    per_block = LANES // head_dim
    tf = t.astype(F32)
    t2 = tf * tf
    lane = lax.broadcasted_iota(jnp.int32, (1, LANES), 1)
    out_row = lax.broadcasted_iota(jnp.int32, (n_heads, LANES), 0)
    out = jnp.zeros((n_heads, LANES), F32)
    for c in range(t.shape[1] // LANES):
        blk = t2[:, c * LANES:(c + 1) * LANES]
        for j in range(per_block):
            in_head = (lane >= j * head_dim) & (lane < (j + 1) * head_dim)
            norm2 = jnp.sum(jnp.where(in_head, blk, 0.0), axis=-1, keepdims=True)
            out = jnp.where(out_row == c * per_block + j,
                            jnp.max(norm2, axis=0, keepdims=True), out)
    return out


def _proj_kernel(x_ref, w32_ref, cw_ref, g_ref, gm_ref, wo32_ref, wu32_ref, wd32_ref,
                 yc_ref, q_ref, k_ref, v_ref, qn_ref, kn_ref, wo_ref, wu_ref, wd_ref,
                 u_ref, w_ref, *, rows, cw, aw, q_scale, head_dim):
    i = pl.program_id(1)
    sub = PROJ_SUB_ROWS

    @pl.when((pl.program_id(0) == 0) & (i == 0))
    def _():
        def cast_rows(t, c):
            r = pl.multiple_of(t * LANES, LANES)
            w_ref[pl.ds(r, LANES), :] = w32_ref[pl.ds(r, LANES), :].astype(BF16)
            return c
        lax.fori_loop(0, w_ref.shape[0] // LANES, cast_rows, 0)

    wo_ref[...] = wo32_ref[...].astype(BF16)
    wu_ref[...] = wu32_ref[...].astype(BF16)
    wd_ref[...] = wd32_ref[...].astype(BF16)

    @pl.when(i == 0)
    def _():
        u_ref[0:8, :] = jnp.zeros((8, cw), F32)
        qn_ref[...] = jnp.zeros_like(qn_ref)
        kn_ref[...] = jnp.zeros_like(kn_ref)

    @pl.when(i > 0)
    def _():
        u_ref[0:8, :] = u_ref[rows:rows + 8, :]

    qn = qn_ref[...]
    kn = kn_ref[...]
    gates = []
    for r0 in range(0, rows, sub):
        xb = x_ref[r0:r0 + sub, :].astype(BF16)

        def proj(c0, width, xb=xb):
            return _dot(xb, w_ref[:, c0:c0 + width])

        qb = (proj(3 * cw, aw) * q_scale).astype(BF16)
        kb = proj(3 * cw + aw, aw).astype(BF16)
        q_ref[r0:r0 + sub, :] = qb
        k_ref[r0:r0 + sub, :] = kb
        v_ref[r0:r0 + sub, :] = proj(3 * cw + 2 * aw, aw).astype(BF16)
        qn = jnp.maximum(qn, _max_head_norm2(qb, head_dim))
        kn = jnp.maximum(kn, _max_head_norm2(kb, head_dim))
        b_gate = proj(0, cw)
        u = proj(cw, cw) * proj(2 * cw, cw)
        u_ref[8 + r0:8 + r0 + sub, :] = u
        gates.append((b_gate, u))
    qn_ref[...] = qn
    kn_ref[...] = kn

    for r0, (b_gate, u) in zip(range(0, rows, sub), gates):
        u1 = u_ref[7 + r0:7 + r0 + sub, :]
        u2 = u_ref[6 + r0:6 + r0 + sub, :]
        y = b_gate * (cw_ref[0:1, :] * u2 + cw_ref[1:2, :] * u1 + cw_ref[2:3, :] * u)
        ms = _group_mean_square(y, gm_ref[...])
        yc_ref[r0:r0 + sub, :] = (y * lax.rsqrt(ms + RMS_EPS) * g_ref[...]).astype(BF16)


def _group_mean_matrix(width, groups):
    gid = jnp.arange(width) // (width // groups)
    return jnp.where(gid[:, None] == gid[None, :], groups / width, 0.0).astype(BF16)


def _proj_call(x, w_in, conv_w, g_conv, w_out, w_up, w_down, cw, aw):
    bsz, seq, d = x.shape
    rows = PROJ_ROWS
    head_dim = aw // ATTN_HEADS
    n_i = seq // rows
    n_steps = bsz * n_i
    d_ff = w_up.shape[1]
    assert w_in.shape[0] % LANES == 0
    assert w_out.shape[0] % (16 * n_steps) == 0 and d_ff % (LANES * n_steps) == 0
    wo_spec = pl.BlockSpec((w_out.shape[0] // n_steps, w_out.shape[1]),
                           lambda b, i: (b * n_i + i, 0))
    wu_spec = pl.BlockSpec((d, d_ff // n_steps), lambda b, i: (0, b * n_i + i))
    wd_spec = pl.BlockSpec((d_ff // n_steps, d), lambda b, i: (b * n_i + i, 0))
    kern = functools.partial(_proj_kernel, rows=rows, cw=cw, aw=aw,
                             q_scale=head_dim ** -0.5, head_dim=head_dim)
    out_spec = pl.BlockSpec((None, rows, cw), lambda b, i: (b, i, 0))
    out_spec_a = pl.BlockSpec((None, rows, aw), lambda b, i: (b, i, 0))
    norm_spec = pl.BlockSpec((None, ATTN_HEADS, LANES), lambda b, i: (b, 0, 0))
    return pl.pallas_call(
        kern,
        grid=(bsz, n_i),
        in_specs=[
            pl.BlockSpec((None, rows, d), lambda b, i: (b, i, 0)),
            pl.BlockSpec(w_in.shape, lambda b, i: (0, 0), pipeline_mode=pl.Buffered(1)),
            pl.BlockSpec(conv_w.shape, lambda b, i: (0, 0)),
            pl.BlockSpec((1, cw), lambda b, i: (0, 0)),
            pl.BlockSpec((cw, cw), lambda b, i: (0, 0)),
            wo_spec, wu_spec, wd_spec,
        ],
        out_specs=[out_spec, out_spec_a, out_spec_a, out_spec_a, norm_spec, norm_spec,
                   wo_spec, wu_spec, wd_spec],
        out_shape=[jax.ShapeDtypeStruct((bsz, seq, cw), BF16)]
        + [jax.ShapeDtypeStruct((bsz, seq, aw), BF16)] * 3
        + [jax.ShapeDtypeStruct((bsz, ATTN_HEADS, LANES), F32)] * 2
        + [jax.ShapeDtypeStruct(w.shape, BF16) for w in (w_out, w_up, w_down)],
        scratch_shapes=[pltpu.VMEM((rows + 8, cw), F32),
                        pltpu.VMEM(w_in.shape, BF16)],
        compiler_params=pltpu.CompilerParams(
            dimension_semantics=("arbitrary", "arbitrary"),
            vmem_limit_bytes=VMEM_LIMIT_BYTES),
        name="proj_conv",
    )(x, w_in, conv_w, g_conv.reshape(1, cw), _group_mean_matrix(cw, CONV_GROUPS),
      w_out, w_up, w_down)


def _attn_kernel(qn_ref, kn_ref, q_ref, k_ref, v_ref, u_ref, cm_ref, gm_ref, g_ref, o_ref,
                 qm_ref, z_ref, hl_ref, lbc_ref, a_ref, carry_ref, acc_ref, oraw_ref,
                 *, tile, head_dim):
    heads = LANES // head_dim
    seq = q_ref.shape[0]
    n_q = seq // tile
    thr = (jnp.max(jnp.sqrt(qn_ref[...] * kn_ref[...])) * LOGIT_BOUND_SLACK
           + UNDERFLOW_MARGIN)
    lane = lax.broadcasted_iota(jnp.int32, (1, LANES), 1)
    in_head = [(lane >= h * head_dim) & (lane < (h + 1) * head_dim)
               for h in range(heads)]
    def mask_heads(t, c):
        r = pl.multiple_of(t * tile, tile)
        q = q_ref[pl.ds(r, tile), :]
        for h in range(heads):
            qm_ref[h, pl.ds(r, tile), :] = jnp.where(in_head[h], q, jnp.zeros_like(q))
        return c

    lax.fori_loop(0, n_q, mask_heads, 0)
    carry_ref[...] = jnp.zeros_like(carry_ref)
    acc_ref[...] = jnp.zeros_like(acc_ref)
    a_ref[...] = jnp.zeros_like(a_ref)
    hl_ref[1] = jnp.zeros(hl_ref.shape[1:], BF16)
    lbc_ref[1] = jnp.full(lbc_ref.shape[1:], MASKED_LOGIT, F32)
    z_ref[1] = jnp.full(z_ref.shape[1:], MASKED_LOGIT, F32)

    def row_start(t):
        return pl.multiple_of(jnp.clip(t, 0, n_q - 1) * tile, tile)

    def next_tile(tq, tk, rest_is_zero):
        new_sweep = (tk == 0) | rest_is_zero
        nq = jnp.minimum(jnp.where(new_sweep, tq + 2, tq), n_q)
        nk = jnp.where(nq >= n_q, n_q, jnp.where(new_sweep, nq, tk - 1))
        return nq, nk

    def stage0(tq, tk, p):
        qm = jnp.concatenate([qm_ref[h, pl.ds(row_start(tq), tile), :]
                              for h in range(heads)], axis=0)
        kt = k_ref[pl.ds(row_start(tk), tile), :]
        z = lax.dot_general(qm, kt, (((1,), (1,)), ((), ())), preferred_element_type=F32)
        z_ref[p] = jnp.minimum(z, cm_ref[(tq == tk).astype(jnp.int32)])

    def stage1(tq, tk, p):
        z = z_ref[1 - p]
        sp = jnp.maximum(z, jnp.log(1.0 + jnp.exp(z)))
        carry = jnp.where(tq == tk, 0.0, carry_ref[1 - p])
        lbc_ref[p] = z - jnp.concatenate([carry] * (tile // LANES), axis=1)
        hl_ref[p] = sp.astype(BF16)
        carry = carry + jnp.sum(sp, axis=-1, keepdims=True)
        carry_ref[1 - p] = carry
        return jnp.min(carry) > thr

    def stage2(p):
        suffix = _dot(hl_ref[1 - p], u_ref[...])
        a_ref[1 - p] = jnp.exp(lbc_ref[1 - p] - suffix).astype(BF16)

    def stage3(tq, tk, p):
        vt = v_ref[pl.ds(row_start(tk), tile), :]
        acc = jnp.where(tq == tk, 0.0, acc_ref[1 - p]) + _dot(a_ref[p], vt)
        acc_ref[1 - p] = acc
        idle = (tq < 0) | (tq >= n_q)
        r3 = pl.multiple_of(jnp.where(idle, n_q, tq) * tile, tile)
        for h in range(heads):
            oraw_ref[h, pl.ds(r3, tile), :] = acc[h * tile:(h + 1) * tile]

    def step(state, p):
        q1, k1, q2, k2, q3, k3, rest_is_zero, _ = state
        q0, k0 = next_tile(q2, k2, rest_is_zero)
        other_rest_is_zero = stage1(q1, k1, p)
        stage2(p)
        stage3(q3, k3, p)
        stage0(q0, k0, p)
        flushed = (q0 >= n_q) & (q1 >= n_q) & (q2 >= n_q)
        return (q0, k0, q1, k1, q2, k2, other_rest_is_zero, flushed)

    def body(state):
        for j in range(ATTN_UNROLL):
            state = step(state, j % 2)
        return state

    def not_started(lane):
        return (jnp.int32(lane - 2), jnp.int32(0))

    lax.while_loop(lambda st: jnp.logical_not(st[-1]), body,
                   not_started(1) + not_started(0) + not_started(1)
                   + (jnp.bool_(False), jnp.bool_(False)))

    def normalise(t, c):
        r = pl.multiple_of(t * NORM_ROWS, NORM_ROWS)
        o = jnp.where(in_head[0], oraw_ref[0, pl.ds(r, NORM_ROWS), :], 0.0)
        for h in range(1, heads):
            o = o + jnp.where(in_head[h], oraw_ref[h, pl.ds(r, NORM_ROWS), :], 0.0)
        ms = _group_mean_square(o, gm_ref[...])
        o_ref[pl.ds(r, NORM_ROWS), :] = (o * lax.rsqrt(ms + RMS_EPS) * g_ref[...]).astype(BF16)
        return c

    lax.fori_loop(0, seq // NORM_ROWS, normalise, 0)


def _attn_call(q, k, v, qn2, kn2, g_attn):
    bsz, seq, aw = q.shape
    tile = ATTN_TILE
    head_dim = aw // ATTN_HEADS
    heads = LANES // head_dim
    rows = heads * tile
    pairs = aw // LANES
    qn2 = qn2.reshape(bsz, pairs, heads, LANES)
    kn2 = kn2.reshape(bsz, pairs, heads, LANES)
    norm_spec = pl.BlockSpec((None, None, heads, LANES), lambda b, p: (b, p, 0, 0))
    kern = functools.partial(_attn_kernel, tile=tile, head_dim=head_dim)
    jj = jnp.arange(tile)
    u = (jj[:, None] >= jj[None, :]).astype(BF16)
    qpos = jnp.arange(rows)[:, None] % tile
    cm = jnp.stack([jnp.full((rows, tile), LOGIT_CAP, F32),
                    jnp.where(jj[None, :] < qpos, LOGIT_CAP, MASKED_LOGIT).astype(F32)])
    seq_spec = pl.BlockSpec((None, seq, LANES), lambda b, p: (b, 0, p))

    def const(shape):
        return pl.BlockSpec(shape, lambda b, p: (0,) * len(shape))

    return pl.pallas_call(
        kern,
        grid=(bsz, pairs),
        in_specs=[norm_spec, norm_spec, seq_spec, seq_spec, seq_spec,
                  const((tile, tile)), const((2, rows, tile)),
                  const((LANES, LANES)),
                  pl.BlockSpec((1, LANES), lambda b, p: (0, p))],
        out_specs=seq_spec,
        scratch_shapes=[pltpu.VMEM((heads, seq, LANES), BF16),
                        pltpu.VMEM((2, rows, tile), F32),
                        pltpu.VMEM((2, rows, tile), BF16),
                        pltpu.VMEM((2, rows, tile), F32),
                        pltpu.VMEM((2, rows, tile), BF16),
                        pltpu.VMEM((2, rows, LANES), F32),
                        pltpu.VMEM((2, rows, LANES), F32),
                        pltpu.VMEM((heads, seq + tile, LANES), F32)],
        out_shape=jax.ShapeDtypeStruct((bsz, seq, aw), BF16),
        compiler_params=pltpu.CompilerParams(
            dimension_semantics=("arbitrary", "arbitrary"),
            vmem_limit_bytes=VMEM_LIMIT_BYTES),
        name="stickbreak_attn",
    )(qn2, kn2, q, k, v, u, cm, _group_mean_matrix(LANES, heads), g_attn.reshape(1, aw))


def _layer_norm(x, g, b):
    mu = jnp.mean(x, axis=-1, keepdims=True)
    xc = x - mu
    var = jnp.mean(xc * xc, axis=-1, keepdims=True)
    return xc * lax.rsqrt(var + LN_EPS) * g + b


def _mlp_kernel(x_ref, yc_ref, ya_ref, wo_ref, g1_ref, b1_ref, wu_ref, wd_ref,
                g2_ref, b2_ref, o_ref, *, cw, alpha, chunk):
    blocks = [slice(r, r + MLP_SUB_ROWS) for r in range(0, x_ref.shape[0], MLP_SUB_ROWS)]
    mix = [_dot(yc_ref[b, :], wo_ref[0:cw, :]) + _dot(ya_ref[b, :], wo_ref[cw:, :])
           for b in blocks]
    for b, mix_b in zip(blocks, mix):
        x1 = _layer_norm(alpha * x_ref[b, :] + mix_b, g1_ref[...], b1_ref[...])
        x1b = x1.astype(BF16)
        ffn = None
        for c0 in range(0, wu_ref.shape[1], chunk):
            hid = jnp.maximum(_dot(x1b, wu_ref[:, c0:c0 + chunk]), 0.0)
            part = _dot((hid * hid).astype(BF16), wd_ref[c0:c0 + chunk, :])
            ffn = part if ffn is None else ffn + part
        o_ref[b, :] = _layer_norm(alpha * x1 + ffn, g2_ref[...], b2_ref[...])


def _mlp_call(x2, yc2, ya2, w_out, ln1_g, ln1_b, w_up, w_down, ln2_g, ln2_b, alpha):
    t, d = x2.shape
    cw = yc2.shape[1]
    aw = ya2.shape[1]
    rows = MLP_ROWS
    kern = functools.partial(_mlp_kernel, cw=cw, alpha=alpha, chunk=MLP_CHUNK)

    def const(shape):
        return pl.BlockSpec(shape, lambda i: (0, 0), pipeline_mode=pl.Buffered(1))

    return pl.pallas_call(
        kern,
        grid=(t // rows,),
        in_specs=[
            pl.BlockSpec((rows, d), lambda i: (i, 0)),
            pl.BlockSpec((rows, cw), lambda i: (i, 0)),
            pl.BlockSpec((rows, aw), lambda i: (i, 0)),
            const(w_out.shape), const((1, d)), const((1, d)),
            const(w_up.shape), const(w_down.shape), const((1, d)), const((1, d)),
        ],
        out_specs=pl.BlockSpec((rows, d), lambda i: (i, 0)),
        out_shape=jax.ShapeDtypeStruct((t, d), F32),
        compiler_params=pltpu.CompilerParams(
            dimension_semantics=("arbitrary",),
            vmem_limit_bytes=VMEM_LIMIT_BYTES),
        name="outproj_mlp",
    )(x2, yc2, ya2, w_out, ln1_g.reshape(1, d), ln1_b.reshape(1, d),
      w_up, w_down, ln2_g.reshape(1, d), ln2_b.reshape(1, d))


def kernel(x, w_in, conv_w, g_conv, g_attn, w_out, ln1_g, ln1_b, w_up, w_down, ln2_g, ln2_b):
    bsz, seq, d = x.shape
    depth = w_in.shape[0]
    cw = conv_w.shape[2]
    aw = g_attn.shape[1]
    assert w_in.shape[2] == 3 * cw + 3 * aw and w_out.shape[1] == cw + aw
    assert aw % LANES == 0 and seq % PROJ_ROWS == 0 and seq % NORM_ROWS == 0
    assert seq % ATTN_TILE == 0 and ATTN_TILE % LANES == 0
    alpha = float((2 * depth) ** 0.25)
    for l in range(depth):
        yc, q, k, v, qn2, kn2, wo, wu, wd = _proj_call(
            x, w_in[l], conv_w[l], g_conv[l], w_out[l], w_up[l], w_down[l], cw, aw)
        ya = _attn_call(q, k, v, qn2, kn2, g_attn[l:l + 1])
        x = _mlp_call(x.reshape(bsz * seq, d), yc.reshape(bsz * seq, cw),
                      ya.reshape(bsz * seq, aw), wo, ln1_g[l], ln1_b[l],
                      wu, wd, ln2_g[l], ln2_b[l], alpha
                      ).reshape(bsz, seq, d)
    return x
```

```python
import functools

import jax
import jax.numpy as jnp
from jax import lax
from jax.experimental import pallas as pl
from jax.experimental.pallas import tpu as pltpu

F32 = jnp.float32
BF16 = jnp.bfloat16

CONV_GROUPS = 8
CONV_K = 3
ATTN_HEADS = 8
LN_EPS = 1e-5
RMS_EPS = 1e-6
LOGIT_CAP = 40.0
MASKED_LOGIT = -1e4
UNDERFLOW_MARGIN = 110.0
LOGIT_BOUND_SLACK = 1.01

LANES = 128
SUBLANES = 8
BF16_TILE_ROWS = 2 * SUBLANES
VMEM_LIMIT_BYTES = 56 * 1024 * 1024

PROJ_ROWS = 1024
PROJ_SUB_ROWS = 256
HALO = SUBLANES
ATTN_TILE = 256
ATTN_UNROLL = 12
NORM_ROWS = 2048
MLP_ROWS = 1024
MLP_SUB_ROWS = 256
MLP_CHUNK = 1024


def _dot(a, b):
    return jnp.dot(a, b, preferred_element_type=F32)


def _group_mean_square(y, gm):
    return _dot((y * y).astype(BF16), gm)


def _max_head_norm2(t, head_dim):
    n_heads = t.shape[1] // head_dim
    per_block = LANES // head_dim
    tf = t.astype(F32)
    t2 = tf * tf
    lane = lax.broadcasted_iota(jnp.int32, (1, LANES), 1)
    out_row = lax.broadcasted_iota(jnp.int32, (n_heads, LANES), 0)
    out = jnp.zeros((n_heads, LANES), F32)
    for c in range(t.shape[1] // LANES):
        blk = t2[:, c * LANES:(c + 1) * LANES]
        for j in range(per_block):
            in_head = (lane >= j * head_dim) & (lane < (j + 1) * head_dim)
            norm2 = jnp.sum(jnp.where(in_head, blk, 0.0), axis=-1, keepdims=True)
            out = jnp.where(out_row == c * per_block + j,
                            jnp.max(norm2, axis=0, keepdims=True), out)
    return out


def _proj_kernel(x_ref, w32_ref, cw_ref, g_ref, gm_ref, wo32_ref, wu32_ref, wd32_ref,
                 yc_ref, q_ref, k_ref, v_ref, qn_ref, kn_ref, wo_ref, wu_ref, wd_ref,
                 u_ref, w_ref, *, rows, cw, aw, q_scale, head_dim):
    i = pl.program_id(1)
    sub = PROJ_SUB_ROWS

    @pl.when((pl.program_id(0) == 0) & (i == 0))
    def _():
        def cast_rows(t, c):
            r = pl.multiple_of(t * LANES, LANES)
            w_ref[pl.ds(r, LANES), :] = w32_ref[pl.ds(r, LANES), :].astype(BF16)
            return c
        lax.fori_loop(0, w_ref.shape[0] // LANES, cast_rows, 0)

    wo_ref[...] = wo32_ref[...].astype(BF16)
    wu_ref[...] = wu32_ref[...].astype(BF16)
    wd_ref[...] = wd32_ref[...].astype(BF16)

    @pl.when(i == 0)
    def _():
        u_ref[0:HALO, :] = jnp.zeros((HALO, cw), F32)
        qn_ref[...] = jnp.zeros_like(qn_ref)
        kn_ref[...] = jnp.zeros_like(kn_ref)

    @pl.when(i > 0)
    def _():
        u_ref[0:HALO, :] = u_ref[rows:rows + HALO, :]

    qn = qn_ref[...]
    kn = kn_ref[...]
    gates = []
    for r0 in range(0, rows, sub):
        xb = x_ref[r0:r0 + sub, :].astype(BF16)

        def proj(c0, width, xb=xb):
            return _dot(xb, w_ref[:, c0:c0 + width])

        qb = (proj(3 * cw, aw) * q_scale).astype(BF16)
        kb = proj(3 * cw + aw, aw).astype(BF16)
        q_ref[r0:r0 + sub, :] = qb
        k_ref[r0:r0 + sub, :] = kb
        v_ref[r0:r0 + sub, :] = proj(3 * cw + 2 * aw, aw).astype(BF16)
        qn = jnp.maximum(qn, _max_head_norm2(qb, head_dim))
        kn = jnp.maximum(kn, _max_head_norm2(kb, head_dim))
        b_gate = proj(0, cw)
        u = proj(cw, cw) * proj(2 * cw, cw)
        u_ref[HALO + r0:HALO + r0 + sub, :] = u
        gates.append((b_gate, u))
    qn_ref[...] = qn
    kn_ref[...] = kn

    for r0, (b_gate, u) in zip(range(0, rows, sub), gates):
        taps = [u_ref[HALO - back + r0:HALO - back + r0 + sub, :]
                for back in range(CONV_K - 1, 0, -1)] + [u]
        conv = cw_ref[0:1, :] * taps[0]
        for j in range(1, CONV_K):
            conv = conv + cw_ref[j:j + 1, :] * taps[j]
        y = b_gate * conv
        ms = _group_mean_square(y, gm_ref[...])
        yc_ref[r0:r0 + sub, :] = (y * lax.rsqrt(ms + RMS_EPS) * g_ref[...]).astype(BF16)


def _group_mean_matrix(width, groups):
    gid = jnp.arange(width) // (width // groups)
    return jnp.where(gid[:, None] == gid[None, :], groups / width, 0.0).astype(BF16)


def _proj_call(x, w_in, conv_w, g_conv, w_out, w_up, w_down, cw, aw):
    bsz, seq, d = x.shape
    rows = PROJ_ROWS
    head_dim = aw // ATTN_HEADS
    n_i = seq // rows
    n_steps = bsz * n_i
    d_ff = w_up.shape[1]
    assert w_in.shape[0] % LANES == 0
    assert w_out.shape[0] % (BF16_TILE_ROWS * n_steps) == 0 and d_ff % (LANES * n_steps) == 0
    wo_spec = pl.BlockSpec((w_out.shape[0] // n_steps, w_out.shape[1]),
                           lambda b, i: (b * n_i + i, 0))
    wu_spec = pl.BlockSpec((d, d_ff // n_steps), lambda b, i: (0, b * n_i + i))
    wd_spec = pl.BlockSpec((d_ff // n_steps, d), lambda b, i: (b * n_i + i, 0))
    kern = functools.partial(_proj_kernel, rows=rows, cw=cw, aw=aw,
                             q_scale=head_dim ** -0.5, head_dim=head_dim)
    out_spec = pl.BlockSpec((None, rows, cw), lambda b, i: (b, i, 0))
    out_spec_a = pl.BlockSpec((None, rows, aw), lambda b, i: (b, i, 0))
    norm_spec = pl.BlockSpec((None, ATTN_HEADS, LANES), lambda b, i: (b, 0, 0))
    return pl.pallas_call(
        kern,
        grid=(bsz, n_i),
        in_specs=[
            pl.BlockSpec((None, rows, d), lambda b, i: (b, i, 0)),
            pl.BlockSpec(w_in.shape, lambda b, i: (0, 0), pipeline_mode=pl.Buffered(1)),
            pl.BlockSpec(conv_w.shape, lambda b, i: (0, 0)),
            pl.BlockSpec((1, cw), lambda b, i: (0, 0)),
            pl.BlockSpec((cw, cw), lambda b, i: (0, 0)),
            wo_spec, wu_spec, wd_spec,
        ],
        out_specs=[out_spec, out_spec_a, out_spec_a, out_spec_a, norm_spec, norm_spec,
                   wo_spec, wu_spec, wd_spec],
        out_shape=[jax.ShapeDtypeStruct((bsz, seq, cw), BF16)]
        + [jax.ShapeDtypeStruct((bsz, seq, aw), BF16)] * 3
        + [jax.ShapeDtypeStruct((bsz, ATTN_HEADS, LANES), F32)] * 2
        + [jax.ShapeDtypeStruct(w.shape, BF16) for w in (w_out, w_up, w_down)],
        scratch_shapes=[pltpu.VMEM((rows + HALO, cw), F32),
                        pltpu.VMEM(w_in.shape, BF16)],
        compiler_params=pltpu.CompilerParams(
            dimension_semantics=("arbitrary", "arbitrary"),
            vmem_limit_bytes=VMEM_LIMIT_BYTES),
        name="proj_conv",
    )(x, w_in, conv_w, g_conv.reshape(1, cw), _group_mean_matrix(cw, CONV_GROUPS),
      w_out, w_up, w_down)


def _attn_kernel(qn_ref, kn_ref, q_ref, k_ref, v_ref, u_ref, cm_ref, gm_ref, g_ref, o_ref,
                 qm_ref, z_ref, hl_ref, lbc_ref, a_ref, carry_ref, acc_ref, oraw_ref,
                 *, tile, head_dim):
    heads = LANES // head_dim
    seq = q_ref.shape[0]
    n_q = seq // tile
    thr = (jnp.max(jnp.sqrt(qn_ref[...] * kn_ref[...])) * LOGIT_BOUND_SLACK
           + UNDERFLOW_MARGIN)
    lane = lax.broadcasted_iota(jnp.int32, (1, LANES), 1)
    in_head = [(lane >= h * head_dim) & (lane < (h + 1) * head_dim)
               for h in range(heads)]
    def mask_heads(t, c):
        r = pl.multiple_of(t * tile, tile)
        q = q_ref[pl.ds(r, tile), :]
        for h in range(heads):
            qm_ref[h, pl.ds(r, tile), :] = jnp.where(in_head[h], q, jnp.zeros_like(q))
        return c

    lax.fori_loop(0, n_q, mask_heads, 0)
    carry_ref[...] = jnp.zeros_like(carry_ref)
    acc_ref[...] = jnp.zeros_like(acc_ref)
    a_ref[...] = jnp.zeros_like(a_ref)
    hl_ref[1] = jnp.zeros(hl_ref.shape[1:], BF16)
    lbc_ref[1] = jnp.full(lbc_ref.shape[1:], MASKED_LOGIT, F32)
    z_ref[1] = jnp.full(z_ref.shape[1:], MASKED_LOGIT, F32)

    def row_start(t):
        return pl.multiple_of(jnp.clip(t, 0, n_q - 1) * tile, tile)

    def next_tile(tq, tk, rest_is_zero):
        new_sweep = (tk == 0) | rest_is_zero
        nq = jnp.minimum(jnp.where(new_sweep, tq + 2, tq), n_q)
        nk = jnp.where(nq >= n_q, n_q, jnp.where(new_sweep, nq, tk - 1))
        return nq, nk

    def stage0(tq, tk, p):
        qm = jnp.concatenate([qm_ref[h, pl.ds(row_start(tq), tile), :]
                              for h in range(heads)], axis=0)
        kt = k_ref[pl.ds(row_start(tk), tile), :]
        z = lax.dot_general(qm, kt, (((1,), (1,)), ((), ())), preferred_element_type=F32)
        z_ref[p] = jnp.minimum(z, cm_ref[(tq == tk).astype(jnp.int32)])

    def stage1(tq, tk, p):
        z = z_ref[1 - p]
        sp = jnp.maximum(z, jnp.log(1.0 + jnp.exp(z)))
        carry = jnp.where(tq == tk, 0.0, carry_ref[1 - p])
        lbc_ref[p] = z - jnp.concatenate([carry] * (tile // LANES), axis=1)
        hl_ref[p] = sp.astype(BF16)
        carry = carry + jnp.sum(sp, axis=-1, keepdims=True)
        carry_ref[1 - p] = carry
        return jnp.min(carry) > thr

    def stage2(p):
        suffix = _dot(hl_ref[1 - p], u_ref[...])
        a_ref[1 - p] = jnp.exp(lbc_ref[1 - p] - suffix).astype(BF16)

    def stage3(tq, tk, p):
        vt = v_ref[pl.ds(row_start(tk), tile), :]
        acc = jnp.where(tq == tk, 0.0, acc_ref[1 - p]) + _dot(a_ref[p], vt)
        acc_ref[1 - p] = acc
        idle = (tq < 0) | (tq >= n_q)
        r3 = pl.multiple_of(jnp.where(idle, n_q, tq) * tile, tile)
        for h in range(heads):
            oraw_ref[h, pl.ds(r3, tile), :] = acc[h * tile:(h + 1) * tile]

    def step(state, p):
        q1, k1, q2, k2, q3, k3, rest_is_zero, _ = state
        q0, k0 = next_tile(q2, k2, rest_is_zero)
        other_rest_is_zero = stage1(q1, k1, p)
        stage2(p)
        stage3(q3, k3, p)
        stage0(q0, k0, p)
        flushed = (q0 >= n_q) & (q1 >= n_q) & (q2 >= n_q)
        return (q0, k0, q1, k1, q2, k2, other_rest_is_zero, flushed)

    def body(state):
        for j in range(ATTN_UNROLL):
            state = step(state, j % 2)
        return state

    def not_started(lane):
        return (jnp.int32(lane - 2), jnp.int32(0))

    lax.while_loop(lambda st: jnp.logical_not(st[-1]), body,
                   not_started(1) + not_started(0) + not_started(1)
                   + (jnp.bool_(False), jnp.bool_(False)))

    def normalise(t, c):
        r = pl.multiple_of(t * NORM_ROWS, NORM_ROWS)
        o = jnp.where(in_head[0], oraw_ref[0, pl.ds(r, NORM_ROWS), :], 0.0)
        for h in range(1, heads):
            o = o + jnp.where(in_head[h], oraw_ref[h, pl.ds(r, NORM_ROWS), :], 0.0)
        ms = _group_mean_square(o, gm_ref[...])
        o_ref[pl.ds(r, NORM_ROWS), :] = (o * lax.rsqrt(ms + RMS_EPS) * g_ref[...]).astype(BF16)
        return c

    lax.fori_loop(0, seq // NORM_ROWS, normalise, 0)


def _attn_call(q, k, v, qn2, kn2, g_attn):
    bsz, seq, aw = q.shape
    tile = ATTN_TILE
    head_dim = aw // ATTN_HEADS
    heads = LANES // head_dim
    rows = heads * tile
    pairs = aw // LANES
    qn2 = qn2.reshape(bsz, pairs, heads, LANES)
    kn2 = kn2.reshape(bsz, pairs, heads, LANES)
    norm_spec = pl.BlockSpec((None, None, heads, LANES), lambda b, p: (b, p, 0, 0))
    kern = functools.partial(_attn_kernel, tile=tile, head_dim=head_dim)
    jj = jnp.arange(tile)
    u = (jj[:, None] >= jj[None, :]).astype(BF16)
    qpos = jnp.arange(rows)[:, None] % tile
    cm = jnp.stack([jnp.full((rows, tile), LOGIT_CAP, F32),
                    jnp.where(jj[None, :] < qpos, LOGIT_CAP, MASKED_LOGIT).astype(F32)])
    seq_spec = pl.BlockSpec((None, seq, LANES), lambda b, p: (b, 0, p))

    def const(shape):
        return pl.BlockSpec(shape, lambda b, p: (0,) * len(shape))

    return pl.pallas_call(
        kern,
        grid=(bsz, pairs),
        in_specs=[norm_spec, norm_spec, seq_spec, seq_spec, seq_spec,
                  const((tile, tile)), const((2, rows, tile)),
                  const((LANES, LANES)),
                  pl.BlockSpec((1, LANES), lambda b, p: (0, p))],
        out_specs=seq_spec,
        scratch_shapes=[pltpu.VMEM((heads, seq, LANES), BF16),
                        pltpu.VMEM((2, rows, tile), F32),
                        pltpu.VMEM((2, rows, tile), BF16),
                        pltpu.VMEM((2, rows, tile), F32),
                        pltpu.VMEM((2, rows, tile), BF16),
                        pltpu.VMEM((2, rows, LANES), F32),
                        pltpu.VMEM((2, rows, LANES), F32),
                        pltpu.VMEM((heads, seq + tile, LANES), F32)],
        out_shape=jax.ShapeDtypeStruct((bsz, seq, aw), BF16),
        compiler_params=pltpu.CompilerParams(
            dimension_semantics=("arbitrary", "arbitrary"),
            vmem_limit_bytes=VMEM_LIMIT_BYTES),
        name="stickbreak_attn",
    )(qn2, kn2, q, k, v, u, cm, _group_mean_matrix(LANES, heads), g_attn.reshape(1, aw))


def _layer_norm(x, g, b):
    mu = jnp.mean(x, axis=-1, keepdims=True)
    xc = x - mu
    var = jnp.mean(xc * xc, axis=-1, keepdims=True)
    return xc * lax.rsqrt(var + LN_EPS) * g + b


def _mlp_kernel(x_ref, yc_ref, ya_ref, wo_ref, g1_ref, b1_ref, wu_ref, wd_ref,
                g2_ref, b2_ref, o_ref, *, cw, alpha, chunk):
    blocks = [slice(r, r + MLP_SUB_ROWS) for r in range(0, x_ref.shape[0], MLP_SUB_ROWS)]
    mix = [_dot(yc_ref[b, :], wo_ref[0:cw, :]) + _dot(ya_ref[b, :], wo_ref[cw:, :])
           for b in blocks]
    for b, mix_b in zip(blocks, mix):
        x1 = _layer_norm(alpha * x_ref[b, :] + mix_b, g1_ref[...], b1_ref[...])
        x1b = x1.astype(BF16)
        ffn = None
        for c0 in range(0, wu_ref.shape[1], chunk):
            hid = jnp.maximum(_dot(x1b, wu_ref[:, c0:c0 + chunk]), 0.0)
            part = _dot((hid * hid).astype(BF16), wd_ref[c0:c0 + chunk, :])
            ffn = part if ffn is None else ffn + part
        o_ref[b, :] = _layer_norm(alpha * x1 + ffn, g2_ref[...], b2_ref[...])


def _mlp_call(x2, yc2, ya2, w_out, ln1_g, ln1_b, w_up, w_down, ln2_g, ln2_b, alpha):
    t, d = x2.shape
    cw = yc2.shape[1]
    aw = ya2.shape[1]
    rows = MLP_ROWS
    kern = functools.partial(_mlp_kernel, cw=cw, alpha=alpha, chunk=MLP_CHUNK)

    def const(shape):
        return pl.BlockSpec(shape, lambda i: (0, 0), pipeline_mode=pl.Buffered(1))

    return pl.pallas_call(
        kern,
        grid=(t // rows,),
        in_specs=[
            pl.BlockSpec((rows, d), lambda i: (i, 0)),
            pl.BlockSpec((rows, cw), lambda i: (i, 0)),
            pl.BlockSpec((rows, aw), lambda i: (i, 0)),
            const(w_out.shape), const((1, d)), const((1, d)),
            const(w_up.shape), const(w_down.shape), const((1, d)), const((1, d)),
        ],
        out_specs=pl.BlockSpec((rows, d), lambda i: (i, 0)),
        out_shape=jax.ShapeDtypeStruct((t, d), F32),
        compiler_params=pltpu.CompilerParams(
            dimension_semantics=("arbitrary",),
            vmem_limit_bytes=VMEM_LIMIT_BYTES),
        name="outproj_mlp",
    )(x2, yc2, ya2, w_out, ln1_g.reshape(1, d), ln1_b.reshape(1, d),
      w_up, w_down, ln2_g.reshape(1, d), ln2_b.reshape(1, d))


def kernel(x, w_in, conv_w, g_conv, g_attn, w_out, ln1_g, ln1_b, w_up, w_down, ln2_g, ln2_b):
    bsz, seq, d = x.shape
    depth = w_in.shape[0]
    cw = conv_w.shape[2]
    aw = g_attn.shape[1]
    assert w_in.shape[2] == 3 * cw + 3 * aw and w_out.shape[1] == cw + aw
    assert conv_w.shape[1] == CONV_K <= HALO + 1
    assert aw % LANES == 0 and seq % PROJ_ROWS == 0 and seq % NORM_ROWS == 0
    assert seq % ATTN_TILE == 0 and ATTN_TILE % LANES == 0
    alpha = float((2 * depth) ** 0.25)
    for l in range(depth):
        yc, q, k, v, qn2, kn2, wo, wu, wd = _proj_call(
            x, w_in[l], conv_w[l], g_conv[l], w_out[l], w_up[l], w_down[l], cw, aw)
        ya = _attn_call(q, k, v, qn2, kn2, g_attn[l:l + 1])
        x = _mlp_call(x.reshape(bsz * seq, d), yc.reshape(bsz * seq, cw),
                      ya.reshape(bsz * seq, aw), wo, ln1_g[l], ln1_b[l],
                      wu, wd, ln2_g[l], ln2_b[l], alpha
                      ).reshape(bsz, seq, d)
    return x
```

```python
import functools

import jax
import jax.numpy as jnp
from jax import lax
from jax.experimental import pallas as pl
from jax.experimental.pallas import tpu as pltpu

F32 = jnp.float32
BF16 = jnp.bfloat16

CONV_GROUPS = 8
CONV_K = 3
ATTN_HEADS = 8
LN_EPS = 1e-5
RMS_EPS = 1e-6
LOGIT_CAP = 40.0
MASKED_LOGIT = -1e4
UNDERFLOW_MARGIN = 110.0
LOGIT_BOUND_SLACK = 1.01

LANES = 128
SUBLANES = 8
BF16_TILE_ROWS = 2 * SUBLANES
VMEM_LIMIT_BYTES = 56 * 1024 * 1024

PROJ_ROWS = 1024
PROJ_SUB_ROWS = 256
HALO = SUBLANES
ATTN_TILE = 256
ATTN_UNROLL = 12
NORM_ROWS = 2048
MLP_ROWS = 1024
MLP_SUB_ROWS = 256
MLP_CHUNK = 1024


def _dot(a, b):
    return jnp.dot(a, b, preferred_element_type=F32)


def _group_mean_square(y, gm):
    return _dot((y * y).astype(BF16), gm)


def _max_head_norm2(t, head_dim):
    n_heads = t.shape[1] // head_dim
    per_block = LANES // head_dim
    tf = t.astype(F32)
    t2 = tf * tf
    lane = lax.broadcasted_iota(jnp.int32, (1, LANES), 1)
    out_row = lax.broadcasted_iota(jnp.int32, (n_heads, LANES), 0)
    out = jnp.zeros((n_heads, LANES), F32)
    for c in range(t.shape[1] // LANES):
        blk = t2[:, c * LANES:(c + 1) * LANES]
        for j in range(per_block):
            in_head = (lane >= j * head_dim) & (lane < (j + 1) * head_dim)
            norm2 = jnp.sum(jnp.where(in_head, blk, 0.0), axis=-1, keepdims=True)
            out = jnp.where(out_row == c * per_block + j,
                            jnp.max(norm2, axis=0, keepdims=True), out)
    return out


def _proj_kernel(x_ref, w32_ref, cw_ref, g_ref, gm_ref, wo32_ref, wu32_ref, wd32_ref,
                 yc_ref, q_ref, k_ref, v_ref, qn_ref, kn_ref, wo_ref, wu_ref, wd_ref,
                 u_ref, w_ref, *, rows, cw, aw, q_scale, head_dim):
    i = pl.program_id(1)
    sub = PROJ_SUB_ROWS

    @pl.when((pl.program_id(0) == 0) & (i == 0))
    def _():
        def cast_rows(t, c):
            r = pl.multiple_of(t * LANES, LANES)
            w_ref[pl.ds(r, LANES), :] = w32_ref[pl.ds(r, LANES), :].astype(BF16)
            return c
        lax.fori_loop(0, w_ref.shape[0] // LANES, cast_rows, 0)

    wo_ref[...] = wo32_ref[...].astype(BF16)
    wu_ref[...] = wu32_ref[...].astype(BF16)
    wd_ref[...] = wd32_ref[...].astype(BF16)

    @pl.when(i == 0)
    def _():
        u_ref[0:HALO, :] = jnp.zeros((HALO, cw), F32)
        qn_ref[...] = jnp.zeros_like(qn_ref)
        kn_ref[...] = jnp.zeros_like(kn_ref)

    @pl.when(i > 0)
    def _():
        u_ref[0:HALO, :] = u_ref[rows:rows + HALO, :]

    qn = qn_ref[...]
    kn = kn_ref[...]
    gates = []
    for r0 in range(0, rows, sub):
        xb = x_ref[r0:r0 + sub, :].astype(BF16)

        def proj(c0, width, xb=xb):
            return _dot(xb, w_ref[:, c0:c0 + width])

        qb = (proj(3 * cw, aw) * q_scale).astype(BF16)
        kb = proj(3 * cw + aw, aw).astype(BF16)
        q_ref[r0:r0 + sub, :] = qb
        k_ref[r0:r0 + sub, :] = kb
        v_ref[r0:r0 + sub, :] = proj(3 * cw + 2 * aw, aw).astype(BF16)
        qn = jnp.maximum(qn, _max_head_norm2(qb, head_dim))
        kn = jnp.maximum(kn, _max_head_norm2(kb, head_dim))
        b_gate = proj(0, cw)
        u = proj(cw, cw) * proj(2 * cw, cw)
        u_ref[HALO + r0:HALO + r0 + sub, :] = u
        gates.append((b_gate, u))
    qn_ref[...] = qn
    kn_ref[...] = kn

    for r0, (b_gate, u) in zip(range(0, rows, sub), gates):
        taps = [u_ref[HALO - back + r0:HALO - back + r0 + sub, :]
                for back in range(CONV_K - 1, 0, -1)] + [u]
        conv = cw_ref[0:1, :] * taps[0]
        for j in range(1, CONV_K):
            conv = conv + cw_ref[j:j + 1, :] * taps[j]
        y = b_gate * conv
        ms = _group_mean_square(y, gm_ref[...])
        yc_ref[r0:r0 + sub, :] = (y * lax.rsqrt(ms + RMS_EPS) * g_ref[...]).astype(BF16)


def _group_mean_matrix(width, groups):
    gid = jnp.arange(width) // (width // groups)
    return jnp.where(gid[:, None] == gid[None, :], groups / width, 0.0).astype(BF16)


def _proj_call(x, w_in, conv_w, g_conv, w_out, w_up, w_down, cw, aw):
    bsz, seq, d = x.shape
    rows = PROJ_ROWS
    head_dim = aw // ATTN_HEADS
    n_i = seq // rows
    n_steps = bsz * n_i
    d_ff = w_up.shape[1]
    assert w_in.shape[0] % LANES == 0
    assert w_out.shape[0] % (BF16_TILE_ROWS * n_steps) == 0 and d_ff % (LANES * n_steps) == 0
    wo_spec = pl.BlockSpec((w_out.shape[0] // n_steps, w_out.shape[1]),
                           lambda b, i: (b * n_i + i, 0))
    wu_spec = pl.BlockSpec((d, d_ff // n_steps), lambda b, i: (0, b * n_i + i))
    wd_spec = pl.BlockSpec((d_ff // n_steps, d), lambda b, i: (b * n_i + i, 0))
    kern = functools.partial(_proj_kernel, rows=rows, cw=cw, aw=aw,
                             q_scale=head_dim ** -0.5, head_dim=head_dim)
    out_spec = pl.BlockSpec((None, rows, cw), lambda b, i: (b, i, 0))
    out_spec_a = pl.BlockSpec((None, rows, aw), lambda b, i: (b, i, 0))
    norm_spec = pl.BlockSpec((None, ATTN_HEADS, LANES), lambda b, i: (b, 0, 0))
    return pl.pallas_call(
        kern,
        grid=(bsz, n_i),
        in_specs=[
            pl.BlockSpec((None, rows, d), lambda b, i: (b, i, 0)),
            pl.BlockSpec(w_in.shape, lambda b, i: (0, 0), pipeline_mode=pl.Buffered(1)),
            pl.BlockSpec(conv_w.shape, lambda b, i: (0, 0)),
            pl.BlockSpec((1, cw), lambda b, i: (0, 0)),
            pl.BlockSpec((cw, cw), lambda b, i: (0, 0)),
            wo_spec, wu_spec, wd_spec,
        ],
        out_specs=[out_spec, out_spec_a, out_spec_a, out_spec_a, norm_spec, norm_spec,
                   wo_spec, wu_spec, wd_spec],
        out_shape=[jax.ShapeDtypeStruct((bsz, seq, cw), BF16)]
        + [jax.ShapeDtypeStruct((bsz, seq, aw), BF16)] * 3
        + [jax.ShapeDtypeStruct((bsz, ATTN_HEADS, LANES), F32)] * 2
        + [jax.ShapeDtypeStruct(w.shape, BF16) for w in (w_out, w_up, w_down)],
        scratch_shapes=[pltpu.VMEM((rows + HALO, cw), F32),
                        pltpu.VMEM(w_in.shape, BF16)],
        compiler_params=pltpu.CompilerParams(
            dimension_semantics=("arbitrary", "arbitrary"),
            vmem_limit_bytes=VMEM_LIMIT_BYTES),
        name="proj_conv",
    )(x, w_in, conv_w, g_conv.reshape(1, cw), _group_mean_matrix(cw, CONV_GROUPS),
      w_out, w_up, w_down)


def _attn_kernel(qn_ref, kn_ref, q_ref, k_ref, v_ref, u_ref, cm_ref, gm_ref, g_ref, o_ref,
                 qm_ref, z_ref, hl_ref, lbc_ref, a_ref, carry_ref, acc_ref, oraw_ref,
                 *, tile, head_dim):
    heads = LANES // head_dim
    seq = q_ref.shape[0]
    n_q = seq // tile
    thr = (jnp.max(jnp.sqrt(qn_ref[...] * kn_ref[...])) * LOGIT_BOUND_SLACK
           + UNDERFLOW_MARGIN)
    lane = lax.broadcasted_iota(jnp.int32, (1, LANES), 1)
    in_head = [(lane >= h * head_dim) & (lane < (h + 1) * head_dim)
               for h in range(heads)]
    def mask_heads(t, c):
        r = pl.multiple_of(t * tile, tile)
        q = q_ref[pl.ds(r, tile), :]
        for h in range(heads):
            qm_ref[h, pl.ds(r, tile), :] = jnp.where(in_head[h], q, jnp.zeros_like(q))
        return c

    lax.fori_loop(0, n_q, mask_heads, 0)
    carry_ref[...] = jnp.zeros_like(carry_ref)
    acc_ref[...] = jnp.zeros_like(acc_ref)
    a_ref[...] = jnp.zeros_like(a_ref)
    hl_ref[1] = jnp.zeros(hl_ref.shape[1:], BF16)
    lbc_ref[1] = jnp.full(lbc_ref.shape[1:], MASKED_LOGIT, F32)
    z_ref[1] = jnp.full(z_ref.shape[1:], MASKED_LOGIT, F32)

    def row_start(t):
        return pl.multiple_of(jnp.clip(t, 0, n_q - 1) * tile, tile)

    def next_tile(tq, tk, rest_is_zero):
        new_sweep = (tk == 0) | rest_is_zero
        nq = jnp.minimum(jnp.where(new_sweep, tq + 2, tq), n_q)
        nk = jnp.where(nq >= n_q, n_q, jnp.where(new_sweep, nq, tk - 1))
        return nq, nk

    def stage0(tq, tk, p):
        qm = jnp.concatenate([qm_ref[h, pl.ds(row_start(tq), tile), :]
                              for h in range(heads)], axis=0)
        kt = k_ref[pl.ds(row_start(tk), tile), :]
        z = lax.dot_general(qm, kt, (((1,), (1,)), ((), ())), preferred_element_type=F32)
        z_ref[p] = jnp.minimum(z, cm_ref[(tq == tk).astype(jnp.int32)])

    def stage1(tq, tk, p):
        z = z_ref[1 - p]
        sp = jnp.log(1.0 + jnp.exp(z))
        carry = jnp.where(tq == tk, 0.0, carry_ref[1 - p])
        lbc_ref[p] = z - jnp.concatenate([carry] * (tile // LANES), axis=1)
        hl_ref[p] = sp.astype(BF16)
        carry = carry + jnp.sum(sp, axis=-1, keepdims=True)
        carry_ref[1 - p] = carry
        return jnp.min(carry) > thr

    def stage2(p):
        suffix = _dot(hl_ref[1 - p], u_ref[...])
        a_ref[1 - p] = jnp.exp(lbc_ref[1 - p] - suffix).astype(BF16)

    def stage3(tq, tk, p):
        vt = v_ref[pl.ds(row_start(tk), tile), :]
        acc = jnp.where(tq == tk, 0.0, acc_ref[1 - p]) + _dot(a_ref[p], vt)
        acc_ref[1 - p] = acc
        idle = (tq < 0) | (tq >= n_q)
        r3 = pl.multiple_of(jnp.where(idle, n_q, tq) * tile, tile)
        for h in range(heads):
            oraw_ref[h, pl.ds(r3, tile), :] = acc[h * tile:(h + 1) * tile]

    def step(state, p):
        q1, k1, q2, k2, q3, k3, rest_is_zero, _ = state
        q0, k0 = next_tile(q2, k2, rest_is_zero)
        other_rest_is_zero = stage1(q1, k1, p)
        stage2(p)
        stage3(q3, k3, p)
        stage0(q0, k0, p)
        flushed = (q0 >= n_q) & (q1 >= n_q) & (q2 >= n_q)
        return (q0, k0, q1, k1, q2, k2, other_rest_is_zero, flushed)

    def body(state):
        for j in range(ATTN_UNROLL):
            state = step(state, j % 2)
        return state

    def not_started(lane):
        return (jnp.int32(lane - 2), jnp.int32(0))

    lax.while_loop(lambda st: jnp.logical_not(st[-1]), body,
                   not_started(1) + not_started(0) + not_started(1)
                   + (jnp.bool_(False), jnp.bool_(False)))

    def normalise(t, c):
        r = pl.multiple_of(t * NORM_ROWS, NORM_ROWS)
        o = jnp.where(in_head[0], oraw_ref[0, pl.ds(r, NORM_ROWS), :], 0.0)
        for h in range(1, heads):
            o = o + jnp.where(in_head[h], oraw_ref[h, pl.ds(r, NORM_ROWS), :], 0.0)
        ms = _group_mean_square(o, gm_ref[...])
        o_ref[pl.ds(r, NORM_ROWS), :] = (o * lax.rsqrt(ms + RMS_EPS) * g_ref[...]).astype(BF16)
        return c

    lax.fori_loop(0, seq // NORM_ROWS, normalise, 0)


def _attn_call(q, k, v, qn2, kn2, g_attn):
    bsz, seq, aw = q.shape
    tile = ATTN_TILE
    head_dim = aw // ATTN_HEADS
    heads = LANES // head_dim
    rows = heads * tile
    pairs = aw // LANES
    qn2 = qn2.reshape(bsz, pairs, heads, LANES)
    kn2 = kn2.reshape(bsz, pairs, heads, LANES)
    norm_spec = pl.BlockSpec((None, None, heads, LANES), lambda b, p: (b, p, 0, 0))
    kern = functools.partial(_attn_kernel, tile=tile, head_dim=head_dim)
    jj = jnp.arange(tile)
    u = (jj[:, None] >= jj[None, :]).astype(BF16)
    qpos = jnp.arange(rows)[:, None] % tile
    cm = jnp.stack([jnp.full((rows, tile), LOGIT_CAP, F32),
                    jnp.where(jj[None, :] < qpos, LOGIT_CAP, MASKED_LOGIT).astype(F32)])
    seq_spec = pl.BlockSpec((None, seq, LANES), lambda b, p: (b, 0, p))

    def const(shape):
        return pl.BlockSpec(shape, lambda b, p: (0,) * len(shape))

    return pl.pallas_call(
        kern,
        grid=(bsz, pairs),
        in_specs=[norm_spec, norm_spec, seq_spec, seq_spec, seq_spec,
                  const((tile, tile)), const((2, rows, tile)),
                  const((LANES, LANES)),
                  pl.BlockSpec((1, LANES), lambda b, p: (0, p))],
        out_specs=seq_spec,
        scratch_shapes=[pltpu.VMEM((heads, seq, LANES), BF16),
                        pltpu.VMEM((2, rows, tile), F32),
                        pltpu.VMEM((2, rows, tile), BF16),
                        pltpu.VMEM((2, rows, tile), F32),
                        pltpu.VMEM((2, rows, tile), BF16),
                        pltpu.VMEM((2, rows, LANES), F32),
                        pltpu.VMEM((2, rows, LANES), F32),
                        pltpu.VMEM((heads, seq + tile, LANES), F32)],
        out_shape=jax.ShapeDtypeStruct((bsz, seq, aw), BF16),
        compiler_params=pltpu.CompilerParams(
            dimension_semantics=("arbitrary", "arbitrary"),
            vmem_limit_bytes=VMEM_LIMIT_BYTES),
        name="stickbreak_attn",
    )(qn2, kn2, q, k, v, u, cm, _group_mean_matrix(LANES, heads), g_attn.reshape(1, aw))


def _layer_norm(x, g, b):
    mu = jnp.mean(x, axis=-1, keepdims=True)
    xc = x - mu
    var = jnp.mean(xc * xc, axis=-1, keepdims=True)
    return xc * lax.rsqrt(var + LN_EPS) * g + b


def _mlp_kernel(x_ref, yc_ref, ya_ref, wo_ref, g1_ref, b1_ref, wu_ref, wd_ref,
                g2_ref, b2_ref, o_ref, *, cw, alpha, chunk):
    blocks = [slice(r, r + MLP_SUB_ROWS) for r in range(0, x_ref.shape[0], MLP_SUB_ROWS)]
    mix = [_dot(yc_ref[b, :], wo_ref[0:cw, :]) + _dot(ya_ref[b, :], wo_ref[cw:, :])
           for b in blocks]
    for b, mix_b in zip(blocks, mix):
        x1 = _layer_norm(alpha * x_ref[b, :] + mix_b, g1_ref[...], b1_ref[...])
        x1b = x1.astype(BF16)
        ffn = None
        for c0 in range(0, wu_ref.shape[1], chunk):
            hid = jnp.maximum(_dot(x1b, wu_ref[:, c0:c0 + chunk]), 0.0)
            part = _dot((hid * hid).astype(BF16), wd_ref[c0:c0 + chunk, :])
            ffn = part if ffn is None else ffn + part
        o_ref[b, :] = _layer_norm(alpha * x1 + ffn, g2_ref[...], b2_ref[...])


def _mlp_call(x2, yc2, ya2, w_out, ln1_g, ln1_b, w_up, w_down, ln2_g, ln2_b, alpha):
    t, d = x2.shape
    cw = yc2.shape[1]
    aw = ya2.shape[1]
    rows = MLP_ROWS
    kern = functools.partial(_mlp_kernel, cw=cw, alpha=alpha, chunk=MLP_CHUNK)

    def const(shape):
        return pl.BlockSpec(shape, lambda i: (0, 0), pipeline_mode=pl.Buffered(1))

    return pl.pallas_call(
        kern,
        grid=(t // rows,),
        in_specs=[
            pl.BlockSpec((rows, d), lambda i: (i, 0)),
            pl.BlockSpec((rows, cw), lambda i: (i, 0)),
            pl.BlockSpec((rows, aw), lambda i: (i, 0)),
            const(w_out.shape), const((1, d)), const((1, d)),
            const(w_up.shape), const(w_down.shape), const((1, d)), const((1, d)),
        ],
        out_specs=pl.BlockSpec((rows, d), lambda i: (i, 0)),
        out_shape=jax.ShapeDtypeStruct((t, d), F32),
        compiler_params=pltpu.CompilerParams(
            dimension_semantics=("arbitrary",),
            vmem_limit_bytes=VMEM_LIMIT_BYTES),
        name="outproj_mlp",
    )(x2, yc2, ya2, w_out, ln1_g.reshape(1, d), ln1_b.reshape(1, d),
      w_up, w_down, ln2_g.reshape(1, d), ln2_b.reshape(1, d))


def kernel(x, w_in, conv_w, g_conv, g_attn, w_out, ln1_g, ln1_b, w_up, w_down, ln2_g, ln2_b):
    bsz, seq, d = x.shape
    depth = w_in.shape[0]
    cw = conv_w.shape[2]
    aw = g_attn.shape[1]
    assert w_in.shape[2] == 3 * cw + 3 * aw and w_out.shape[1] == cw + aw
    assert conv_w.shape[1] == CONV_K <= HALO + 1
    assert aw % LANES == 0 and seq % PROJ_ROWS == 0 and seq % NORM_ROWS == 0
    assert seq % ATTN_TILE == 0 and ATTN_TILE % LANES == 0
    alpha = float((2 * depth) ** 0.25)
    for l in range(depth):
        yc, q, k, v, qn2, kn2, wo, wu, wd = _proj_call(
            x, w_in[l], conv_w[l], g_conv[l], w_out[l], w_up[l], w_down[l], cw, aw)
        ya = _attn_call(q, k, v, qn2, kn2, g_attn[l:l + 1])
        x = _mlp_call(x.reshape(bsz * seq, d), yc.reshape(bsz * seq, cw),
                      ya.reshape(bsz * seq, aw), wo, ln1_g[l], ln1_b[l],
                      wu, wd, ln2_g[l], ln2_b[l], alpha
                      ).reshape(bsz, seq, d)
    return x
```

```python
import functools

import jax
import jax.numpy as jnp
from jax import lax
from jax.experimental import pallas as pl
from jax.experimental.pallas import tpu as pltpu

F32 = jnp.float32
BF16 = jnp.bfloat16

CONV_GROUPS = 8
CONV_K = 3
ATTN_HEADS = 8
LN_EPS = 1e-5
RMS_EPS = 1e-6
LOGIT_CAP = 40.0
MASKED_LOGIT = -1e4
UNDERFLOW_MARGIN = 110.0
LOGIT_BOUND_SLACK = 1.01

LANES = 128
SUBLANES = 8
BF16_TILE_ROWS = 2 * SUBLANES
VMEM_LIMIT_BYTES = 56 * 1024 * 1024

PROJ_ROWS = 1024
PROJ_SUB_ROWS = 256
HALO = SUBLANES
ATTN_TILE = 256
ATTN_UNROLL = 18
NORM_ROWS = 2048
MLP_ROWS = 1024
MLP_SUB_ROWS = 256
MLP_CHUNK = 1024


def _dot(a, b):
    return jnp.dot(a, b, preferred_element_type=F32)


def _group_mean_square(y, gm):
    return _dot((y * y).astype(BF16), gm)


def _max_head_norm2(t, head_dim):
    n_heads = t.shape[1] // head_dim
    per_block = LANES // head_dim
    tf = t.astype(F32)
    t2 = tf * tf
    lane = lax.broadcasted_iota(jnp.int32, (1, LANES), 1)
    out_row = lax.broadcasted_iota(jnp.int32, (n_heads, LANES), 0)
    out = jnp.zeros((n_heads, LANES), F32)
    for c in range(t.shape[1] // LANES):
        blk = t2[:, c * LANES:(c + 1) * LANES]
        for j in range(per_block):
            in_head = (lane >= j * head_dim) & (lane < (j + 1) * head_dim)
            norm2 = jnp.sum(jnp.where(in_head, blk, 0.0), axis=-1, keepdims=True)
            out = jnp.where(out_row == c * per_block + j,
                            jnp.max(norm2, axis=0, keepdims=True), out)
    return out


def _proj_kernel(x_ref, w32_ref, cw_ref, g_ref, gm_ref, wo32_ref, wu32_ref, wd32_ref,
                 yc_ref, q_ref, k_ref, v_ref, qn_ref, kn_ref, wo_ref, wu_ref, wd_ref,
                 u_ref, w_ref, *, rows, cw, aw, q_scale, head_dim):
    i = pl.program_id(1)
    sub = PROJ_SUB_ROWS

    @pl.when((pl.program_id(0) == 0) & (i == 0))
    def _():
        def cast_rows(t, c):
            r = pl.multiple_of(t * LANES, LANES)
            w_ref[pl.ds(r, LANES), :] = w32_ref[pl.ds(r, LANES), :].astype(BF16)
            return c
        lax.fori_loop(0, w_ref.shape[0] // LANES, cast_rows, 0)

    wo_ref[...] = wo32_ref[...].astype(BF16)
    wu_ref[...] = wu32_ref[...].astype(BF16)
    wd_ref[...] = wd32_ref[...].astype(BF16)

    @pl.when(i == 0)
    def _():
        u_ref[0:HALO, :] = jnp.zeros((HALO, cw), F32)
        qn_ref[...] = jnp.zeros_like(qn_ref)
        kn_ref[...] = jnp.zeros_like(kn_ref)

    @pl.when(i > 0)
    def _():
        u_ref[0:HALO, :] = u_ref[rows:rows + HALO, :]

    qn = qn_ref[...]
    kn = kn_ref[...]
    gates = []
    for r0 in range(0, rows, sub):
        xb = x_ref[r0:r0 + sub, :].astype(BF16)

        def proj(c0, width, xb=xb):
            return _dot(xb, w_ref[:, c0:c0 + width])

        qb = (proj(3 * cw, aw) * q_scale).astype(BF16)
        kb = proj(3 * cw + aw, aw).astype(BF16)
        q_ref[r0:r0 + sub, :] = qb
        k_ref[r0:r0 + sub, :] = kb
        v_ref[r0:r0 + sub, :] = proj(3 * cw + 2 * aw, aw).astype(BF16)
        qn = jnp.maximum(qn, _max_head_norm2(qb, head_dim))
        kn = jnp.maximum(kn, _max_head_norm2(kb, head_dim))
        b_gate = proj(0, cw)
        u = proj(cw, cw) * proj(2 * cw, cw)
        u_ref[HALO + r0:HALO + r0 + sub, :] = u
        gates.append((b_gate, u))
    qn_ref[...] = qn
    kn_ref[...] = kn

    for r0, (b_gate, u) in zip(range(0, rows, sub), gates):
        taps = [u_ref[HALO - back + r0:HALO - back + r0 + sub, :]
                for back in range(CONV_K - 1, 0, -1)] + [u]
        conv = cw_ref[0:1, :] * taps[0]
        for j in range(1, CONV_K):
            conv = conv + cw_ref[j:j + 1, :] * taps[j]
        y = b_gate * conv
        ms = _group_mean_square(y, gm_ref[...])
        yc_ref[r0:r0 + sub, :] = (y * lax.rsqrt(ms + RMS_EPS) * g_ref[...]).astype(BF16)


def _group_mean_matrix(width, groups):
    gid = jnp.arange(width) // (width // groups)
    return jnp.where(gid[:, None] == gid[None, :], groups / width, 0.0).astype(BF16)


def _proj_call(x, w_in, conv_w, g_conv, w_out, w_up, w_down, cw, aw):
    bsz, seq, d = x.shape
    rows = PROJ_ROWS
    head_dim = aw // ATTN_HEADS
    n_i = seq // rows
    n_steps = bsz * n_i
    d_ff = w_up.shape[1]
    assert w_in.shape[0] % LANES == 0
    assert w_out.shape[0] % (BF16_TILE_ROWS * n_steps) == 0 and d_ff % (LANES * n_steps) == 0
    wo_spec = pl.BlockSpec((w_out.shape[0] // n_steps, w_out.shape[1]),
                           lambda b, i: (b * n_i + i, 0))
    wu_spec = pl.BlockSpec((d, d_ff // n_steps), lambda b, i: (0, b * n_i + i))
    wd_spec = pl.BlockSpec((d_ff // n_steps, d), lambda b, i: (b * n_i + i, 0))
    kern = functools.partial(_proj_kernel, rows=rows, cw=cw, aw=aw,
                             q_scale=head_dim ** -0.5, head_dim=head_dim)
    out_spec = pl.BlockSpec((None, rows, cw), lambda b, i: (b, i, 0))
    out_spec_a = pl.BlockSpec((None, rows, aw), lambda b, i: (b, i, 0))
    norm_spec = pl.BlockSpec((None, ATTN_HEADS, LANES), lambda b, i: (b, 0, 0))
    return pl.pallas_call(
        kern,
        grid=(bsz, n_i),
        in_specs=[
            pl.BlockSpec((None, rows, d), lambda b, i: (b, i, 0)),
            pl.BlockSpec(w_in.shape, lambda b, i: (0, 0), pipeline_mode=pl.Buffered(1)),
            pl.BlockSpec(conv_w.shape, lambda b, i: (0, 0)),
            pl.BlockSpec((1, cw), lambda b, i: (0, 0)),
            pl.BlockSpec((cw, cw), lambda b, i: (0, 0)),
            wo_spec, wu_spec, wd_spec,
        ],
        out_specs=[out_spec, out_spec_a, out_spec_a, out_spec_a, norm_spec, norm_spec,
                   wo_spec, wu_spec, wd_spec],
        out_shape=[jax.ShapeDtypeStruct((bsz, seq, cw), BF16)]
        + [jax.ShapeDtypeStruct((bsz, seq, aw), BF16)] * 3
        + [jax.ShapeDtypeStruct((bsz, ATTN_HEADS, LANES), F32)] * 2
        + [jax.ShapeDtypeStruct(w.shape, BF16) for w in (w_out, w_up, w_down)],
        scratch_shapes=[pltpu.VMEM((rows + HALO, cw), F32),
                        pltpu.VMEM(w_in.shape, BF16)],
        compiler_params=pltpu.CompilerParams(
            dimension_semantics=("arbitrary", "arbitrary"),
            vmem_limit_bytes=VMEM_LIMIT_BYTES),
        name="proj_conv",
    )(x, w_in, conv_w, g_conv.reshape(1, cw), _group_mean_matrix(cw, CONV_GROUPS),
      w_out, w_up, w_down)


def _attn_kernel(qn_ref, kn_ref, q_ref, k_ref, v_ref, u_ref, cm_ref, gm_ref, g_ref, o_ref,
                 qm_ref, z_ref, hl_ref, lbc_ref, a_ref, carry_ref, acc_ref, oraw_ref,
                 *, tile, head_dim):
    heads = LANES // head_dim
    seq = q_ref.shape[0]
    n_q = seq // tile
    thr = (jnp.max(jnp.sqrt(qn_ref[...] * kn_ref[...])) * LOGIT_BOUND_SLACK
           + UNDERFLOW_MARGIN)
    lane = lax.broadcasted_iota(jnp.int32, (1, LANES), 1)
    in_head = [(lane >= h * head_dim) & (lane < (h + 1) * head_dim)
               for h in range(heads)]
    def mask_heads(t, c):
        r = pl.multiple_of(t * tile, tile)
        q = q_ref[pl.ds(r, tile), :]
        for h in range(heads):
            qm_ref[h, pl.ds(r, tile), :] = jnp.where(in_head[h], q, jnp.zeros_like(q))
        return c

    lax.fori_loop(0, n_q, mask_heads, 0)
    carry_ref[...] = jnp.zeros_like(carry_ref)
    acc_ref[...] = jnp.zeros_like(acc_ref)
    a_ref[...] = jnp.zeros_like(a_ref)
    hl_ref[1] = jnp.zeros(hl_ref.shape[1:], BF16)
    lbc_ref[1] = jnp.full(lbc_ref.shape[1:], MASKED_LOGIT, F32)
    z_ref[1] = jnp.full(z_ref.shape[1:], MASKED_LOGIT, F32)

    def row_start(t):
        return pl.multiple_of(jnp.clip(t, 0, n_q - 1) * tile, tile)

    def next_tile(tq, tk, rest_is_zero):
        new_sweep = (tk == 0) | rest_is_zero
        nq = jnp.minimum(jnp.where(new_sweep, tq + 2, tq), n_q)
        nk = jnp.where(nq >= n_q, n_q, jnp.where(new_sweep, nq, tk - 1))
        return nq, nk

    def stage0(tq, tk, p):
        qm = jnp.concatenate([qm_ref[h, pl.ds(row_start(tq), tile), :]
                              for h in range(heads)], axis=0)
        kt = k_ref[pl.ds(row_start(tk), tile), :]
        z = lax.dot_general(qm, kt, (((1,), (1,)), ((), ())), preferred_element_type=F32)
        z_ref[p] = jnp.minimum(z, cm_ref[(tq == tk).astype(jnp.int32)])

    def stage1(tq, tk, p):
        z = z_ref[1 - p]
        sp = jnp.log(1.0 + jnp.exp(z))
        carry = jnp.where(tq == tk, 0.0, carry_ref[1 - p])
        lbc_ref[p] = z - jnp.concatenate([carry] * (tile // LANES), axis=1)
        hl_ref[p] = sp.astype(BF16)
        carry = carry + jnp.sum(sp, axis=-1, keepdims=True)
        carry_ref[1 - p] = carry
        return jnp.min(carry) > thr

    def stage2(p):
        suffix = _dot(hl_ref[1 - p], u_ref[...])
        a_ref[1 - p] = jnp.exp(lbc_ref[1 - p] - suffix).astype(BF16)

    def stage3(tq, tk, p):
        vt = v_ref[pl.ds(row_start(tk), tile), :]
        acc = jnp.where(tq == tk, 0.0, acc_ref[1 - p]) + _dot(a_ref[p], vt)
        acc_ref[1 - p] = acc
        idle = (tq < 0) | (tq >= n_q)
        r3 = pl.multiple_of(jnp.where(idle, n_q, tq) * tile, tile)
        for h in range(heads):
            oraw_ref[h, pl.ds(r3, tile), :] = acc[h * tile:(h + 1) * tile]

    def step(state, p):
        q1, k1, q2, k2, q3, k3, rest_is_zero, _ = state
        q0, k0 = next_tile(q2, k2, rest_is_zero)
        other_rest_is_zero = stage1(q1, k1, p)
        stage2(p)
        stage3(q3, k3, p)
        stage0(q0, k0, p)
        flushed = (q0 >= n_q) & (q1 >= n_q) & (q2 >= n_q)
        return (q0, k0, q1, k1, q2, k2, other_rest_is_zero, flushed)

    def body(state):
        for j in range(ATTN_UNROLL):
            state = step(state, j % 2)
        return state

    def not_started(lane):
        return (jnp.int32(lane - 2), jnp.int32(0))

    lax.while_loop(lambda st: jnp.logical_not(st[-1]), body,
                   not_started(1) + not_started(0) + not_started(1)
                   + (jnp.bool_(False), jnp.bool_(False)))

    def normalise(t, c):
        r = pl.multiple_of(t * NORM_ROWS, NORM_ROWS)
        o = jnp.where(in_head[0], oraw_ref[0, pl.ds(r, NORM_ROWS), :], 0.0)
        for h in range(1, heads):
            o = o + jnp.where(in_head[h], oraw_ref[h, pl.ds(r, NORM_ROWS), :], 0.0)
        ms = _group_mean_square(o, gm_ref[...])
        o_ref[pl.ds(r, NORM_ROWS), :] = (o * lax.rsqrt(ms + RMS_EPS) * g_ref[...]).astype(BF16)
        return c

    lax.fori_loop(0, seq // NORM_ROWS, normalise, 0)


def _attn_call(q, k, v, qn2, kn2, g_attn):
    bsz, seq, aw = q.shape
    tile = ATTN_TILE
    head_dim = aw // ATTN_HEADS
    heads = LANES // head_dim
    rows = heads * tile
    pairs = aw // LANES
    qn2 = qn2.reshape(bsz, pairs, heads, LANES)
    kn2 = kn2.reshape(bsz, pairs, heads, LANES)
    norm_spec = pl.BlockSpec((None, None, heads, LANES), lambda b, p: (b, p, 0, 0))
    kern = functools.partial(_attn_kernel, tile=tile, head_dim=head_dim)
    jj = jnp.arange(tile)
    u = (jj[:, None] >= jj[None, :]).astype(BF16)
    qpos = jnp.arange(rows)[:, None] % tile
    cm = jnp.stack([jnp.full((rows, tile), LOGIT_CAP, F32),
                    jnp.where(jj[None, :] < qpos, LOGIT_CAP, MASKED_LOGIT).astype(F32)])
    seq_spec = pl.BlockSpec((None, seq, LANES), lambda b, p: (b, 0, p))

    def const(shape):
        return pl.BlockSpec(shape, lambda b, p: (0,) * len(shape))

    return pl.pallas_call(
        kern,
        grid=(bsz, pairs),
        in_specs=[norm_spec, norm_spec, seq_spec, seq_spec, seq_spec,
                  const((tile, tile)), const((2, rows, tile)),
                  const((LANES, LANES)),
                  pl.BlockSpec((1, LANES), lambda b, p: (0, p))],
        out_specs=seq_spec,
        scratch_shapes=[pltpu.VMEM((heads, seq, LANES), BF16),
                        pltpu.VMEM((2, rows, tile), F32),
                        pltpu.VMEM((2, rows, tile), BF16),
                        pltpu.VMEM((2, rows, tile), F32),
                        pltpu.VMEM((2, rows, tile), BF16),
                        pltpu.VMEM((2, rows, LANES), F32),
                        pltpu.VMEM((2, rows, LANES), F32),
                        pltpu.VMEM((heads, seq + tile, LANES), F32)],
        out_shape=jax.ShapeDtypeStruct((bsz, seq, aw), BF16),
        compiler_params=pltpu.CompilerParams(
            dimension_semantics=("arbitrary", "arbitrary"),
            vmem_limit_bytes=VMEM_LIMIT_BYTES),
        name="stickbreak_attn",
    )(qn2, kn2, q, k, v, u, cm, _group_mean_matrix(LANES, heads), g_attn.reshape(1, aw))


def _layer_norm(x, g, b):
    mu = jnp.mean(x, axis=-1, keepdims=True)
    xc = x - mu
    var = jnp.mean(xc * xc, axis=-1, keepdims=True)
    return xc * lax.rsqrt(var + LN_EPS) * g + b


def _mlp_kernel(x_ref, yc_ref, ya_ref, wo_ref, g1_ref, b1_ref, wu_ref, wd_ref,
                g2_ref, b2_ref, o_ref, *, cw, alpha, chunk):
    blocks = [slice(r, r + MLP_SUB_ROWS) for r in range(0, x_ref.shape[0], MLP_SUB_ROWS)]
    mix = [_dot(yc_ref[b, :], wo_ref[0:cw, :]) + _dot(ya_ref[b, :], wo_ref[cw:, :])
           for b in blocks]
    for b, mix_b in zip(blocks, mix):
        x1 = _layer_norm(alpha * x_ref[b, :] + mix_b, g1_ref[...], b1_ref[...])
        x1b = x1.astype(BF16)
        ffn = None
        for c0 in range(0, wu_ref.shape[1], chunk):
            hid = jnp.maximum(_dot(x1b, wu_ref[:, c0:c0 + chunk]), 0.0)
            part = _dot((hid * hid).astype(BF16), wd_ref[c0:c0 + chunk, :])
            ffn = part if ffn is None else ffn + part
        o_ref[b, :] = _layer_norm(alpha * x1 + ffn, g2_ref[...], b2_ref[...])


def _mlp_call(x2, yc2, ya2, w_out, ln1_g, ln1_b, w_up, w_down, ln2_g, ln2_b, alpha):
    t, d = x2.shape
    cw = yc2.shape[1]
    aw = ya2.shape[1]
    rows = MLP_ROWS
    kern = functools.partial(_mlp_kernel, cw=cw, alpha=alpha, chunk=MLP_CHUNK)

    def const(shape):
        return pl.BlockSpec(shape, lambda i: (0, 0), pipeline_mode=pl.Buffered(1))

    return pl.pallas_call(
        kern,
        grid=(t // rows,),
        in_specs=[
            pl.BlockSpec((rows, d), lambda i: (i, 0)),
            pl.BlockSpec((rows, cw), lambda i: (i, 0)),
            pl.BlockSpec((rows, aw), lambda i: (i, 0)),
            const(w_out.shape), const((1, d)), const((1, d)),
            const(w_up.shape), const(w_down.shape), const((1, d)), const((1, d)),
        ],
        out_specs=pl.BlockSpec((rows, d), lambda i: (i, 0)),
        out_shape=jax.ShapeDtypeStruct((t, d), F32),
        compiler_params=pltpu.CompilerParams(
            dimension_semantics=("arbitrary",),
            vmem_limit_bytes=VMEM_LIMIT_BYTES),
        name="outproj_mlp",
    )(x2, yc2, ya2, w_out, ln1_g.reshape(1, d), ln1_b.reshape(1, d),
      w_up, w_down, ln2_g.reshape(1, d), ln2_b.reshape(1, d))


def kernel(x, w_in, conv_w, g_conv, g_attn, w_out, ln1_g, ln1_b, w_up, w_down, ln2_g, ln2_b):
    bsz, seq, d = x.shape
    depth = w_in.shape[0]
    cw = conv_w.shape[2]
    aw = g_attn.shape[1]
    assert w_in.shape[2] == 3 * cw + 3 * aw and w_out.shape[1] == cw + aw
    assert conv_w.shape[1] == CONV_K <= HALO + 1
    assert aw % LANES == 0 and seq % PROJ_ROWS == 0 and seq % NORM_ROWS == 0
    assert seq % ATTN_TILE == 0 and ATTN_TILE % LANES == 0
    alpha = float((2 * depth) ** 0.25)
    for l in range(depth):
        yc, q, k, v, qn2, kn2, wo, wu, wd = _proj_call(
            x, w_in[l], conv_w[l], g_conv[l], w_out[l], w_up[l], w_down[l], cw, aw)
        ya = _attn_call(q, k, v, qn2, kn2, g_attn[l:l + 1])
        x = _mlp_call(x.reshape(bsz * seq, d), yc.reshape(bsz * seq, cw),
                      ya.reshape(bsz * seq, aw), wo, ln1_g[l], ln1_b[l],
                      wu, wd, ln2_g[l], ln2_b[l], alpha
                      ).reshape(bsz, seq, d)
    return x
```
